```python
import math, functools
import jax, jax.numpy as jnp
from jax import lax
import numpy as np

D_MODEL = 1024
BATCH = 8
SEQ = 2048
DEPTH = 4
DEC_BATCH = 128
DEC_SEQ = 4
PAST_LEN = 2048
PAGE_SIZE = 128

HEAD_DIM = 64
MIX_HEADS = D_MODEL // HEAD_DIM
HA = MIX_HEADS // 4
HC = (MIX_HEADS - HA) // 2
HB = MIX_HEADS - HA - HC
QK_C = HEAD_DIM // 2
ROPE_THETA = 10000.0
QBLOCK = 128
N_GROUPS = 4
EXPERTS_PER_GROUP = 4
N_EXPERTS = N_GROUPS * EXPERTS_PER_GROUP
TOP_K = 2
D_EXPERT = D_MODEL // 4
FORGET_BIAS = 2.0
EPS = 1e-6

kernel_name = "hymba_style_sb_fox_diff_hmoe_step"


def proj_sizes():
    return [HA * HEAD_DIM] * 3 + [HB * HEAD_DIM] * 3 + [HB] + [HC * 2 * QK_C] * 2 + [HC * HEAD_DIM]


def rmsnorm(x, g):
    xf = x.astype(jnp.float32)
    y = xf * lax.rsqrt(jnp.mean(xf * xf, axis=-1, keepdims=True) + EPS)
    return (y * g.astype(jnp.float32)).astype(x.dtype)


def rope(x, pos):
    half = x.shape[-1] // 2
    inv = ROPE_THETA ** (-jnp.arange(half, dtype=jnp.float32) / half)
    ang = pos.astype(jnp.float32)[:, None] * inv[None, :]
    cos = jnp.cos(ang)[None, :, None, None, :]
    sin = jnp.sin(ang)[None, :, None, None, :]
    xf = x.astype(jnp.float32)
    x1, x2 = xf[..., :half], xf[..., half:]
    return jnp.concatenate([x1 * cos - x2 * sin, x2 * cos + x1 * sin], axis=-1).astype(x.dtype)


def block_sweep(attend, q_parts, k_parts, q_offset):
    tq = q_parts[0].shape[1]
    outs = []
    for b0 in range(0, tq, QBLOCK):
        b1 = min(b0 + QBLOCK, tq)
        kend = q_offset + b1
        q_pos = q_offset + jnp.arange(b0, b1)
        k_pos = jnp.arange(kend)
        outs.append(attend(*[t[:, b0:b1] for t in q_parts], *[t[:, :kend] for t in k_parts], q_pos, k_pos))
    return jnp.concatenate(outs, axis=1)


def stick_breaking_attend(q, k, v, q_pos, k_pos):
    z = jnp.einsum('bqhd,bkhd->bhqk', q, k).astype(jnp.float32) * (HEAD_DIM ** -0.5)
    strict = k_pos[None, :] < q_pos[:, None]
    log_keep = jnp.where(strict, jax.nn.log_sigmoid(-z), 0.0)
    later = lax.cumsum(log_keep, axis=3, reverse=True) - log_keep
    w = jnp.where(strict, jnp.exp(jax.nn.log_sigmoid(z) + later), 0.0)
    return jnp.einsum('bhqk,bkhd->bqhd', w.astype(v.dtype), v)


def forgetting_attend(q, fq, k, v, fk, q_pos, k_pos):
    s = jnp.einsum('bqhd,bkhd->bhqk', q, k).astype(jnp.float32) * (HEAD_DIM ** -0.5)
    s = s + jnp.swapaxes(fq, 1, 2)[..., :, None] - jnp.swapaxes(fk, 1, 2)[..., None, :]
    causal = k_pos[None, :] <= q_pos[:, None]
    p = jax.nn.softmax(jnp.where(causal, s, -jnp.inf), axis=-1)
    return jnp.einsum('bhqk,bkhd->bqhd', p.astype(v.dtype), v)


def differential_attend(q, k, v, q_pos, k_pos, lam):
    s = jnp.einsum('bqhmd,bkhmd->bmhqk', q, k).astype(jnp.float32) * (QK_C ** -0.5)
    causal = k_pos[None, :] <= q_pos[:, None]
    p = jax.nn.softmax(jnp.where(causal, s, -jnp.inf), axis=-1)
    w = p[:, 0] - lam * p[:, 1]
    return jnp.einsum('bhqk,bkhd->bqhd', w.astype(v.dtype), v)


def hier_moe(h, w_gr, b_gr, w_er, b_er, w1, w3, w2):
    nb, nt, d = h.shape
    x = h.reshape(nb * nt, d)
    g_logits = jnp.einsum('nd,dg->ng', x, w_gr).astype(jnp.float32) + b_gr.astype(jnp.float32)
    g_prob = jax.nn.softmax(g_logits, axis=-1)
    g_idx = jnp.argmax(g_logits, axis=-1)
    g_top = jnp.take_along_axis(g_prob, g_idx[:, None], axis=1)
    e_logits = (jnp.einsum('nd,de->ne', x, w_er).astype(jnp.float32) + b_er.astype(jnp.float32))
    e_logits = e_logits.reshape(-1, N_GROUPS, EXPERTS_PER_GROUP)
    e_in = jnp.take_along_axis(e_logits, g_idx[:, None, None], axis=1)[:, 0]
    top_p, top_i = lax.top_k(jax.nn.softmax(e_in, axis=-1), TOP_K)
    top_p = top_p / jnp.sum(top_p, axis=-1, keepdims=True) * g_top
    expert_id = g_idx[:, None] * EXPERTS_PER_GROUP + top_i
    gate = jnp.sum(jax.nn.one_hot(expert_id, N_EXPERTS, dtype=jnp.float32) * top_p[..., None], axis=1)
    a = jnp.einsum('nd,edf->nef', x, w1)
    b = jnp.einsum('nd,edf->nef', x, w3)
    hid = jax.nn.silu(a) * b * gate[..., None].astype(x.dtype)
    y = jnp.einsum('nef,efd->nd', hid, w2)
    return y.reshape(nb, nt, d)


def decoder_layer(x, pos, past, lam, lam_init, w_in, b_f, w_out, g_attn, d_norm, g_ffn,
                  w_gr, b_gr, w_er, b_er, w1, w3, w2):
    nb, nt, _ = x.shape
    h = rmsnorm(x, g_attn)
    p = jnp.einsum('btd,dp->btp', h, w_in)
    splits = [int(i) for i in np.cumsum(proj_sizes())[:-1]]
    qa, ka, va, qb, kb, vb, fb, qc, kc, vc = jnp.split(p, splits, axis=-1)
    heads = lambda t, n: t.reshape(nb, nt, n, -1)
    qa, ka, va = heads(qa, HA), heads(ka, HA), heads(va, HA)
    qb, kb, vb = heads(qb, HB), heads(kb, HB), heads(vb, HB)
    logf = jax.nn.log_sigmoid(fb.astype(jnp.float32) + b_f.astype(jnp.float32))
    qc = rope(qc.reshape(nb, nt, HC, 2, QK_C), pos)
    kc = rope(kc.reshape(nb, nt, HC, 2, QK_C), pos)
    vc = heads(vc, HC)
    new_rows = (ka, va, kb, vb, logf, kc.reshape(nb, nt, HC, 2 * QK_C), vc)
    if past is None:
        off = 0
        ka_all, va_all, kb_all, vb_all, lf_all, kc_all, vc_all = ka, va, kb, vb, logf, kc, vc
    else:
        pa_k, pa_v, pb_k, pb_v, pb_lf, pc_k, pc_v = past
        off = pa_k.shape[1]
        cat = lambda a, b: jnp.concatenate([a, b], axis=1)
        ka_all, va_all = cat(pa_k, ka), cat(pa_v, va)
        kb_all, vb_all = cat(pb_k, kb), cat(pb_v, vb)
        lf_all = cat(pb_lf.astype(jnp.float32), logf)
        kc_all = cat(pc_k.reshape(nb, off, HC, 2, QK_C), kc)
        vc_all = cat(pc_v, vc)
    f_all = jnp.cumsum(lf_all, axis=1)
    fq = f_all[:, off:]
    oa = block_sweep(stick_breaking_attend, (qa,), (ka_all, va_all), off)
    ob = block_sweep(forgetting_attend, (qb, fq), (kb_all, vb_all, f_all), off)
    oc = block_sweep(functools.partial(differential_attend, lam=lam), (qc,), (kc_all, vc_all), off)
    oc = rmsnorm(oc, d_norm) * (1.0 - lam_init)
    o = jnp.concatenate([oa.reshape(nb, nt, -1), ob.reshape(nb, nt, -1), oc.reshape(nb, nt, -1)], axis=-1)
    x = x + jnp.einsum('btc,cd->btd', o, w_out)
    x = x + hier_moe(rmsnorm(x, g_ffn), w_gr, b_gr, w_er, b_er, w1, w3, w2)
    return x, new_rows


def gather_pages(cache, l, page_table):
    g = cache[l, page_table]
    return g.reshape(g.shape[0], g.shape[1] * g.shape[2], *g.shape[3:])


def setup_inputs(seed: int = 0) -> dict:
    key = jax.random.key(seed)
    ks = jax.random.split(key, 32)
    n_pages = PAST_LEN // PAGE_SIZE
    n_used = DEC_BATCH * n_pages
    n_pool = n_used + n_used // 4
    p_in = sum(proj_sizes())
    nrm = lambda k, shape, scale=1.0: jax.random.normal(k, shape, jnp.float32) * scale
    pool = (DEPTH, n_pool, PAGE_SIZE)
    inputs = {
        "x_prompt": nrm(ks[0], (BATCH, SEQ, D_MODEL)),
        "x_sample": nrm(ks[1], (DEC_BATCH, DEC_SEQ, D_MODEL)),
        "cache_a_k": nrm(ks[2], pool + (HA, HEAD_DIM)),
        "cache_a_v": nrm(ks[3], pool + (HA, HEAD_DIM)),
        "cache_b_k": nrm(ks[4], pool + (HB, HEAD_DIM)),
        "cache_b_v": nrm(ks[5], pool + (HB, HEAD_DIM)),
        "cache_b_logf": jax.nn.log_sigmoid(FORGET_BIAS + nrm(ks[6], pool + (HB,), 0.5)),
        "cache_c_k": nrm(ks[7], pool + (HC, 2 * QK_C)),
        "cache_c_v": nrm(ks[8], pool + (HC, HEAD_DIM)),
        "page_table": jax.random.permutation(ks[9], n_pool)[:n_used].reshape(DEC_BATCH, n_pages).astype(jnp.int32),
        "w_in": nrm(ks[10], (DEPTH, D_MODEL, p_in), D_MODEL ** -0.5),
        "b_forget": FORGET_BIAS + nrm(ks[11], (DEPTH, HB), 0.5),
        "w_out": nrm(ks[12], (DEPTH, D_MODEL, D_MODEL), D_MODEL ** -0.5),
        "g_attn": 1.0 + nrm(ks[13], (DEPTH, D_MODEL), 0.05),
        "diff_lambda": nrm(ks[14], (DEPTH, 4, QK_C), 0.1),
        "diff_norm": 1.0 + nrm(ks[15], (DEPTH, HEAD_DIM), 0.05),
        "g_ffn": 1.0 + nrm(ks[16], (DEPTH, D_MODEL), 0.05),
        "w_group": nrm(ks[17], (DEPTH, D_MODEL, N_GROUPS), D_MODEL ** -0.5),
        "b_group": nrm(ks[18], (DEPTH, N_GROUPS), 0.01),
        "w_expert_router": nrm(ks[19], (DEPTH, D_MODEL, N_EXPERTS), D_MODEL ** -0.5),
        "b_expert_router": nrm(ks[20], (DEPTH, N_EXPERTS), 0.01),
        "w_up_gate": nrm(ks[21], (DEPTH, N_EXPERTS, D_MODEL, D_EXPERT), D_MODEL ** -0.5),
        "w_up": nrm(ks[22], (DEPTH, N_EXPERTS, D_MODEL, D_EXPERT), D_MODEL ** -0.5),
        "w_down": nrm(ks[23], (DEPTH, N_EXPERTS, D_EXPERT, D_MODEL), D_EXPERT ** -0.5),
        "g_final": 1.0 + nrm(ks[24], (D_MODEL,), 0.05),
    }
    return inputs


def reference(x_prompt, x_sample, cache_a_k, cache_a_v, cache_b_k, cache_b_v, cache_b_logf, cache_c_k, cache_c_v,
              page_table, w_in, b_forget, w_out, g_attn, diff_lambda, diff_norm, g_ffn, w_group, b_group,
              w_expert_router, b_expert_router, w_up_gate, w_up, w_down, g_final):
    n_seq = x_prompt.shape[1]
    n_dec = x_sample.shape[1]
    past_len = page_table.shape[1] * cache_a_k.shape[2]
    pos_p = jnp.arange(n_seq)
    pos_s = past_len + jnp.arange(n_dec)
    xp, xs = x_prompt, x_sample
    rows_p = [[] for _ in range(7)]
    rows_s = [[] for _ in range(7)]
    caches = (cache_a_k, cache_a_v, cache_b_k, cache_b_v, cache_b_logf, cache_c_k, cache_c_v)
    for l in range(DEPTH):
        lam_init = 0.8 - 0.6 * math.exp(-0.3 * l)
        dl = diff_lambda[l].astype(jnp.float32)
        lam = jnp.exp(jnp.sum(dl[0] * dl[1])) - jnp.exp(jnp.sum(dl[2] * dl[3])) + lam_init
        layer_w = (w_in[l], b_forget[l], w_out[l], g_attn[l], diff_norm[l], g_ffn[l], w_group[l], b_group[l],
                   w_expert_router[l], b_expert_router[l], w_up_gate[l], w_up[l], w_down[l])
        xp, new_p = decoder_layer(xp, pos_p, None, lam, lam_init, *layer_w)
        past = tuple(gather_pages(c, l, page_table) for c in caches)
        xs, new_s = decoder_layer(xs, pos_s, past, lam, lam_init, *layer_w)
        for i in range(7):
            rows_p[i].append(new_p[i])
            rows_s[i].append(new_s[i])
    y_prompt = rmsnorm(xp, g_final)
    y_sample = rmsnorm(xs, g_final)
    pa_k, pa_v, pb_k, pb_v, pb_logf, pc_k, pc_v = [jnp.stack(r, axis=0) for r in rows_p]
    sa_k, sa_v, sb_k, sb_v, sb_logf, sc_k, sc_v = [jnp.stack(r, axis=0) for r in rows_s]
    return (y_prompt, y_sample, pa_k, pa_v, pb_k, pb_v, pb_logf, pc_k, pc_v,
            sa_k, sa_v, sb_k, sb_v, sb_logf, sc_k, sc_v)
```

```python
import functools
import math

import jax
import jax.numpy as jnp
from jax import lax
from jax.experimental import pallas as pl
from jax.experimental.pallas import tpu as pltpu

F32 = jnp.float32
BF16 = jnp.bfloat16

HEAD_DIM = 64
HA, HB, HC = 4, 6, 6
QK_C = HEAD_DIM // 2
WA, WB, WC = HA * HEAD_DIM, HB * HEAD_DIM, HC * HEAD_DIM
ROPE_THETA = 10000.0
N_GROUPS = 4
EXPERTS_PER_GROUP = 4
EPS = 1e-6
LANES = 128
HEADS_PER_BLOCK = LANES // HEAD_DIM
VMEM_LIMIT = 56 * 1024 * 1024
NEG_INF = float("-inf")

_C_QA, _C_KA, _C_VA = 0, WA, 2 * WA
_C_QB = 3 * WA
_C_KB, _C_VB = _C_QB + WB, _C_QB + 2 * WB
_C_QC = _C_QB + 3 * WB
_C_KC, _C_VC = _C_QC + WC, _C_QC + 2 * WC
_C_QCR, _C_KCR = _C_QC + 3 * WC, _C_QC + 4 * WC
_C_F = _C_QC + 5 * WC
_P_ARR = _C_F + LANES


def _dot(a, b):
    return jnp.dot(a, b, preferred_element_type=F32)


def _dot_nt(a, b):
    return lax.dot_general(a, b, (((1,), (1,)), ((), ())), preferred_element_type=F32)


def _log_sigmoid(z):
    return jnp.minimum(z, 0.0) - jnp.log(1.0 + jnp.exp(-jnp.abs(z)))


def _split3(x):
    hi = x.astype(BF16)
    r = x - hi.astype(F32)
    mid = r.astype(BF16)
    lo = (r - mid.astype(F32)).astype(BF16)
    return hi, mid, lo


def _dot3(x, m):
    hi, mid, lo = _split3(x)
    return _dot(hi, m) + _dot(mid, m) + _dot(lo, m)


def _inproj_kernel(x_ref, g_ref, w_ref, bf_ref, cos_ref, sin_ref,
                   q_ref, ka_ref, va_ref, kb_ref, vb_ref, lf_ref, lfp_ref, kc_ref, vc_ref):
    x = x_ref[...]
    ms = jnp.mean(x * x, axis=-1, keepdims=True)
    h = (x * lax.rsqrt(ms + EPS) * g_ref[...]).astype(BF16)

    def proj(c0, width):
        return _dot(h, w_ref[:, c0:c0 + width])

    scale_ab = HEAD_DIM ** -0.5
    scale_c = QK_C ** -0.5
    q_ref[:, 0:WA] = (proj(_C_QA, WA) * scale_ab).astype(BF16)
    ka_ref[...] = proj(_C_KA, WA)
    va_ref[...] = proj(_C_VA, WA)
    q_ref[:, WA:WA + WB] = (proj(_C_QB, WB) * scale_ab).astype(BF16)
    kb_ref[...] = proj(_C_KB, WB)
    vb_ref[...] = proj(_C_VB, WB)
    cos = cos_ref[...]
    sin = sin_ref[...]
    qc = proj(_C_QC, WC) * cos + proj(_C_QCR, WC) * sin
    q_ref[:, WA + WB:] = (qc * scale_c).astype(BF16)
    kc_ref[...] = proj(_C_KC, WC) * cos + proj(_C_KCR, WC) * sin
    vc_ref[...] = proj(_C_VC, WC)
    lf = _log_sigmoid(proj(_C_F, LANES) + bf_ref[...])
    lane = lax.broadcasted_iota(jnp.int32, lf.shape, 1)
    lfp_ref[...] = jnp.where(lane < HB, lf, 0.0)
    lf_ref[...] = lf[:, :HB]


def _inproj(x2, g, w_arr, bf_pad, cos_t, sin_t, *, tm, n_pos_tiles):
    n, d = x2.shape
    grid = (n // tm,)
    row = lambda i: (i, 0)
    const = lambda i: (0, 0)
    pos = lambda i: (i % n_pos_tiles, 0)
    out_shape = (
        jax.ShapeDtypeStruct((n, WA + WB + WC), BF16),
        jax.ShapeDtypeStruct((n, WA), F32), jax.ShapeDtypeStruct((n, WA), F32),
        jax.ShapeDtypeStruct((n, WB), F32), jax.ShapeDtypeStruct((n, WB), F32),
        jax.ShapeDtypeStruct((n, HB), F32), jax.ShapeDtypeStruct((n, LANES), F32),
        jax.ShapeDtypeStruct((n, WC), F32), jax.ShapeDtypeStruct((n, WC), F32),
    )
    out_specs = tuple(pl.BlockSpec((tm, s.shape[1]), row) for s in out_shape)
    return pl.pallas_call(
        _inproj_kernel,
        grid=grid,
        in_specs=[
            pl.BlockSpec((tm, d), row),
            pl.BlockSpec((1, d), const),
            pl.BlockSpec((d, _P_ARR), const),
            pl.BlockSpec((1, LANES), const),
            pl.BlockSpec((tm, WC), pos),
            pl.BlockSpec((tm, WC), pos),
        ],
        out_specs=out_specs,
        out_shape=out_shape,
        compiler_params=pltpu.CompilerParams(dimension_semantics=("arbitrary",),
                                             vmem_limit_bytes=VMEM_LIMIT),
        name="inproj",
    )(x2, g, w_arr, bf_pad, cos_t, sin_t)


def _tile_masks(tq):
    row = lax.broadcasted_iota(jnp.int32, (tq, tq), 0)
    col = lax.broadcasted_iota(jnp.int32, (tq, tq), 1)
    return row, col


def _attn_a_kernel(q_ref, k_ref, v_ref, o_ref, kb_ref, vb_ref, *, tq):
    qi = pl.program_id(2)

    @pl.when(qi == 0)
    def _():
        kb_ref[...] = k_ref[0].astype(BF16)
        vb_ref[...] = v_ref[0].astype(BF16)

    q = q_ref[0]
    lane = lax.broadcasted_iota(jnp.int32, q.shape, 1)
    zero = jnp.zeros_like(q)
    qs = (jnp.where(lane < HEAD_DIM, q, zero), jnp.where(lane >= HEAD_DIM, q, zero))
    row, col = _tile_masks(tq)
    strict = col < row
    later = (row > col).astype(BF16)

    def tile(kj, carry, masked):
        off = pl.multiple_of(kj * tq, tq)
        kt = kb_ref[pl.ds(off, tq), :]
        vt = vb_ref[pl.ds(off, tq), :]
        new = []
        for hh in range(HEADS_PER_BLOCK):
            c, acc = carry[hh]
            z = _dot_nt(qs[hh], kt)
            ls = _log_sigmoid(z)
            lk = ls - z
            if masked:
                lk = jnp.where(strict, lk, 0.0)
            hi = lk.astype(BF16)
            lo = (lk - hi.astype(F32)).astype(BF16)
            cum = _dot(hi, later) + _dot(lo, later)
            w = jnp.exp(ls + cum + c)
            if masked:
                w = jnp.where(strict, w, 0.0)
            acc = acc + _dot(w.astype(BF16), vt)
            c = c + jnp.sum(lk, axis=1, keepdims=True)
            new.append((c, acc))
        return tuple(new)

    init = tuple((jnp.zeros((tq, 1), F32), jnp.zeros((tq, LANES), F32)) for _ in range(HEADS_PER_BLOCK))
    carry = tile(qi, init, True)
    carry = lax.fori_loop(0, qi, lambda it, cr: tile(qi - 1 - it, cr, False), carry)
    lane_f = lax.broadcasted_iota(jnp.int32, (tq, LANES), 1)
    o_ref[0] = jnp.where(lane_f < HEAD_DIM, carry[0][1], carry[1][1]).astype(BF16)


def _attn_b_kernel(q_ref, k_ref, v_ref, lf_ref, o_ref, kb_ref, vb_ref, fcol_ref, frow_ref, *, tq, n_t):
    pair = pl.program_id(1)
    qi = pl.program_id(2)
    row, col = _tile_masks(tq)

    @pl.when(qi == 0)
    def _():
        kb_ref[...] = k_ref[0].astype(BF16)
        vb_ref[...] = v_ref[0].astype(BF16)
        incl = (col <= row).astype(BF16)
        for hh in range(HEADS_PER_BLOCK):
            hg = pair * HEADS_PER_BLOCK + hh
            carry = jnp.zeros((1, LANES), F32)
            for c in range(n_t // tq):
                lf = lf_ref[0, c * tq:(c + 1) * tq, :]
                lane = lax.broadcasted_iota(jnp.int32, lf.shape, 1)
                colv = jnp.sum(jnp.where(lane == hg, lf, 0.0), axis=1, keepdims=True)
                colb = jnp.broadcast_to(colv, (tq, LANES))
                f = _cumsum_rows(colb, incl) + carry
                carry = f[tq - 1:tq, :]
                fcol_ref[hh, c * tq:(c + 1) * tq, :] = f
                frow_ref[hh, :, c * tq:(c + 1) * tq] = f.T[0:8, :]

    q = q_ref[0]
    lane = lax.broadcasted_iota(jnp.int32, q.shape, 1)
    zero = jnp.zeros_like(q)
    qs = (jnp.where(lane < HEAD_DIM, q, zero), jnp.where(lane >= HEAD_DIM, q, zero))
    causal = col <= row
    qoff = pl.multiple_of(qi * tq, tq)
    fqs = []
    for hh in range(HEADS_PER_BLOCK):
        fq = fcol_ref[hh, pl.ds(qoff, tq), :]
        fqs.append(jnp.concatenate([fq] * (tq // LANES), axis=1))

    def tile(kj, carry, masked):
        off = pl.multiple_of(kj * tq, tq)
        kt = kb_ref[pl.ds(off, tq), :]
        vt = vb_ref[pl.ds(off, tq), :]
        new = []
        for hh in range(HEADS_PER_BLOCK):
            m, l, acc = carry[hh]
            fk = frow_ref[hh, 0:1, pl.ds(off, tq)]
            s = _dot_nt(qs[hh], kt) + fqs[hh] - fk
            if masked:
                s = jnp.where(causal, s, NEG_INF)
            m_new = jnp.maximum(m, jnp.max(s, axis=1, keepdims=True))
            alpha = jnp.exp(m - m_new)
            p = jnp.exp(s - m_new)
            l = alpha * l + jnp.sum(p, axis=1, keepdims=True)
            acc = alpha * acc + _dot(p.astype(BF16), vt)
            new.append((m_new, l, acc))
        return tuple(new)

    init = tuple((jnp.full((tq, 1), NEG_INF, F32), jnp.zeros((tq, 1), F32), jnp.zeros((tq, LANES), F32))
                 for _ in range(HEADS_PER_BLOCK))
    carry = tile(qi, init, True)
    carry = lax.fori_loop(0, qi, lambda it, cr: tile(qi - 1 - it, cr, False), carry)
    lane_f = lax.broadcasted_iota(jnp.int32, (tq, LANES), 1)
    o0 = carry[0][2] / carry[0][1]
    o1 = carry[1][2] / carry[1][1]
    o_ref[0] = jnp.where(lane_f < HEAD_DIM, o0, o1).astype(BF16)


def _cumsum_rows(x, incl):
    hi, mid, lo = _split3(x)
    return _dot(incl, hi) + _dot(incl, mid) + _dot(incl, lo)


def _diff_lambda(dl_ref, lam_init):
    dl = dl_ref[...]
    a = jnp.sum(dl[0:1, :] * dl[1:2, :], axis=1, keepdims=True)
    b = jnp.sum(dl[2:3, :] * dl[3:4, :], axis=1, keepdims=True)
    return jnp.exp(a) - jnp.exp(b) + lam_init


def _attn_c_kernel(q_ref, k_ref, v_ref, dl_ref, dn_ref, o_ref, kb_ref, vb_ref, *, tq, lam_init):
    qi = pl.program_id(2)

    @pl.when(qi == 0)
    def _():
        kb_ref[...] = k_ref[0].astype(BF16)
        vb_ref[...] = v_ref[0].astype(BF16)

    q = q_ref[0]
    lane = lax.broadcasted_iota(jnp.int32, q.shape, 1)
    zero = jnp.zeros_like(q)
    n_maps = LANES // QK_C
    qs = tuple(jnp.where((lane >= mp * QK_C) & (lane < (mp + 1) * QK_C), q, zero) for mp in range(n_maps))
    row, col = _tile_masks(tq)
    causal = col <= row

    def tile(kj, carry, masked):
        off = pl.multiple_of(kj * tq, tq)
        kt = kb_ref[pl.ds(off, tq), :]
        vt = vb_ref[pl.ds(off, tq), :]
        new = []
        for mp in range(n_maps):
            m, l, acc = carry[mp]
            s = _dot_nt(qs[mp], kt)
            if masked:
                s = jnp.where(causal, s, NEG_INF)
            m_new = jnp.maximum(m, jnp.max(s, axis=1, keepdims=True))
            alpha = jnp.exp(m - m_new)
            p = jnp.exp(s - m_new)
            l = alpha * l + jnp.sum(p, axis=1, keepdims=True)
            acc = alpha * acc + _dot(p.astype(BF16), vt)
            new.append((m_new, l, acc))
        return tuple(new)

    init = tuple((jnp.full((tq, 1), NEG_INF, F32), jnp.zeros((tq, 1), F32), jnp.zeros((tq, LANES), F32))
                 for _ in range(n_maps))
    carry = tile(qi, init, True)
    carry = lax.fori_loop(0, qi, lambda it, cr: tile(qi - 1 - it, cr, False), carry)
    lam = _diff_lambda(dl_ref, lam_init)
    lane_f = lax.broadcasted_iota(jnp.int32, (tq, LANES), 1)
    outs = []
    for hh in range(HEADS_PER_BLOCK):
        m1, m2 = carry[2 * hh], carry[2 * hh + 1]
        o = m1[2] / m1[1] - lam * (m2[2] / m2[1])
        mine = (lane_f >= hh * HEAD_DIM) & (lane_f < (hh + 1) * HEAD_DIM)
        ms = jnp.sum(jnp.where(mine, o * o, 0.0), axis=1, keepdims=True) * (1.0 / HEAD_DIM)
        outs.append(o * lax.rsqrt(ms + EPS))
    o = jnp.where(lane_f < HEAD_DIM, outs[0], outs[1])
    o_ref[0] = (o * dn_ref[...] * (1.0 - lam_init)).astype(BF16)


def _prompt_attn(kind, q_all, k, v, extra, *, tq, lam_init=None):
    nb, nt, w = k.shape
    n_pairs = w // LANES
    q_blk0 = {"a": 0, "b": WA // LANES, "c": (WA + WB) // LANES}[kind]
    grid = (nb, n_pairs, nt // tq)
    q_spec = pl.BlockSpec((1, tq, LANES), lambda b, p, i: (b, i, q_blk0 + p))
    kv_spec = pl.BlockSpec((1, nt, LANES), lambda b, p, i: (b, 0, p))
    o_spec = pl.BlockSpec((1, tq, LANES), lambda b, p, i: (b, i, p))
    scratch = [pltpu.VMEM((nt, LANES), BF16), pltpu.VMEM((nt, LANES), BF16)]
    in_specs = [q_spec, kv_spec, kv_spec]
    if kind == "a":
        body = functools.partial(_attn_a_kernel, tq=tq)
    elif kind == "b":
        body = functools.partial(_attn_b_kernel, tq=tq, n_t=nt)
        in_specs.append(pl.BlockSpec((1, nt, LANES), lambda b, p, i: (b, 0, 0)))
        scratch += [pltpu.VMEM((HEADS_PER_BLOCK, nt, LANES), F32), pltpu.VMEM((HEADS_PER_BLOCK, 8, nt), F32)]
    else:
        body = functools.partial(_attn_c_kernel, tq=tq, lam_init=lam_init)
        in_specs += [pl.BlockSpec((4, QK_C), lambda b, p, i: (0, 0)),
                     pl.BlockSpec((1, LANES), lambda b, p, i: (0, 0))]
    return pl.pallas_call(
        body,
        grid=grid,
        in_specs=in_specs,
        out_specs=o_spec,
        out_shape=jax.ShapeDtypeStruct((nb, nt, w), BF16),
        scratch_shapes=scratch,
        compiler_params=pltpu.CompilerParams(dimension_semantics=("arbitrary", "arbitrary", "arbitrary"),
                                             vmem_limit_bytes=VMEM_LIMIT),
        name="prompt_attn_" + kind,
    )(q_all, k, v, *extra)


_R_EXP0 = 8
_R_ROWS = 32


def _router_rows(lt):
    g = [lt[j:j + 1, :] for j in range(N_GROUPS)]
    gmax = functools.reduce(jnp.maximum, g)
    idx = jnp.full(gmax.shape, N_GROUPS - 1, jnp.int32)
    for j in range(N_GROUPS - 2, -1, -1):
        idx = jnp.where(g[j] == gmax, j, idx)
    g_top = 1.0 / functools.reduce(jnp.add, [jnp.exp(gj - gmax) for gj in g])
    e_in = []
    for j in range(EXPERTS_PER_GROUP):
        ej = lt[_R_EXP0 + (N_GROUPS - 1) * EXPERTS_PER_GROUP + j:_R_EXP0 + (N_GROUPS - 1) * EXPERTS_PER_GROUP + j + 1, :]
        for gg in range(N_GROUPS - 2, -1, -1):
            r = _R_EXP0 + gg * EXPERTS_PER_GROUP + j
            ej = jnp.where(idx == gg, lt[r:r + 1, :], ej)
        e_in.append(ej)
    emax = functools.reduce(jnp.maximum, e_in)
    u = [jnp.exp(ej - emax) for ej in e_in]
    usum = functools.reduce(jnp.add, u)
    p = [uj / usum for uj in u]
    top1 = functools.reduce(jnp.maximum, p)
    i1 = jnp.full(top1.shape, EXPERTS_PER_GROUP - 1, jnp.int32)
    for j in range(EXPERTS_PER_GROUP - 2, -1, -1):
        i1 = jnp.where(p[j] == top1, j, i1)
    rest = [jnp.where(i1 == j, -1.0, p[j]) for j in range(EXPERTS_PER_GROUP)]
    top2 = functools.reduce(jnp.maximum, rest)
    i2 = jnp.full(top2.shape, EXPERTS_PER_GROUP - 1, jnp.int32)
    for j in range(EXPERTS_PER_GROUP - 2, -1, -1):
        i2 = jnp.where(rest[j] == top2, j, i2)
    denom = top1 + top2
    gw = [jnp.where((i1 == j) | (i2 == j), p[j] / denom * g_top, 0.0) for j in range(EXPERTS_PER_GROUP)]
    return gw, idx


def _outproj_moe_kernel(x_ref, oa_ref, ob_ref, oc_ref, wo_ref, gf_ref, wr_ref, br_ref, w1_ref, w3_ref, w2_ref,
                        out_ref, xmid_ref, h2_ref, gate_ref, y_ref):
    g = pl.program_id(1)
    tm = x_ref.shape[0]

    @pl.when(g == 0)
    def _():
        xm = (x_ref[...] + _dot(oa_ref[...], wo_ref[0:WA, :]) + _dot(ob_ref[...], wo_ref[WA:WA + WB, :])
              + _dot(oc_ref[...], wo_ref[WA + WB:, :]))
        xmid_ref[...] = xm
        ms = jnp.mean(xm * xm, axis=-1, keepdims=True)
        h2 = xm * lax.rsqrt(ms + EPS) * gf_ref[...]
        h2_ref[...] = h2.astype(BF16)
        lt = lax.dot_general(wr_ref[...], h2, (((1,), (1,)), ((), ())), precision=lax.Precision.HIGHEST,
                             preferred_element_type=F32) + br_ref[...]
        gw, idx = _router_rows(lt)
        sub = lax.broadcasted_iota(jnp.int32, (8, tm), 0)
        rows = jnp.where(sub == EXPERTS_PER_GROUP, idx.astype(F32), 0.0)
        for j in range(EXPERTS_PER_GROUP):
            rows = jnp.where(sub == j, gw[j], rows)
        full = jnp.concatenate([rows, jnp.zeros((LANES - 8, tm), F32)], axis=0)
        gate_ref[...] = full.T
        y_ref[...] = jnp.zeros_like(y_ref)

    h2b = h2_ref[...]
    gate = gate_ref[...]
    mine = gate[:, EXPERTS_PER_GROUP:EXPERTS_PER_GROUP + 1] == g.astype(F32)
    y = y_ref[...]
    for e in range(EXPERTS_PER_GROUP):
        a = _dot(h2b, w1_ref[0, e])
        b = _dot(h2b, w3_ref[0, e])
        gcol = jnp.where(mine, gate[:, e:e + 1], 0.0)
        hid = a * (1.0 / (1.0 + jnp.exp(-a))) * b * gcol
        y = y + _dot(hid.astype(BF16), w2_ref[0, e])
    y_ref[...] = y

    @pl.when(g == N_GROUPS - 1)
    def _():
        out_ref[...] = xmid_ref[...] + y_ref[...]


def _outproj_moe(x2, oa, ob, oc, wo, gf, wr_t, br, w1, w3, w2, layer, *, tm):
    n, d = x2.shape
    f = w1.shape[-1]
    grid = (n // tm, N_GROUPS)
    row = lambda i, g: (i, 0)
    const = lambda i, g: (0, 0)
    wsel = lambda i, g: (layer, g, 0, 0)
    return pl.pallas_call(
        _outproj_moe_kernel,
        grid=grid,
        in_specs=[
            pl.BlockSpec((tm, d), row),
            pl.BlockSpec((tm, WA), row), pl.BlockSpec((tm, WB), row), pl.BlockSpec((tm, WC), row),
            pl.BlockSpec((d, d), const),
            pl.BlockSpec((1, d), const),
            pl.BlockSpec((_R_ROWS, d), const),
            pl.BlockSpec((_R_ROWS, 1), const),
            pl.BlockSpec((1, EXPERTS_PER_GROUP, d, f), wsel),
            pl.BlockSpec((1, EXPERTS_PER_GROUP, d, f), wsel),
            pl.BlockSpec((1, EXPERTS_PER_GROUP, f, d), wsel),
        ],
        out_specs=pl.BlockSpec((tm, d), row),
        out_shape=jax.ShapeDtypeStruct((n, d), F32),
        scratch_shapes=[pltpu.VMEM((tm, d), F32), pltpu.VMEM((tm, d), BF16), pltpu.VMEM((tm, LANES), F32),
                        pltpu.VMEM((tm, d), F32)],
        compiler_params=pltpu.CompilerParams(dimension_semantics=("arbitrary", "arbitrary"),
                                             vmem_limit_bytes=VMEM_LIMIT),
        name="outproj_moe",
    )(x2, oa, ob, oc, wo, gf, wr_t, br, w1, w3, w2)


def _final_norm_kernel(x_ref, g_ref, o_ref):
    x = x_ref[...]
    ms = jnp.mean(x * x, axis=-1, keepdims=True)
    o_ref[...] = x * lax.rsqrt(ms + EPS) * g_ref[...]


def _final_norm(x2, g, *, tm):
    n, d = x2.shape
    return pl.pallas_call(
        _final_norm_kernel,
        grid=(n // tm,),
        in_specs=[pl.BlockSpec((tm, d), lambda i: (i, 0)), pl.BlockSpec((1, d), lambda i: (0, 0))],
        out_specs=pl.BlockSpec((tm, d), lambda i: (i, 0)),
        out_shape=jax.ShapeDtypeStruct((n, d), F32),
        compiler_params=pltpu.CompilerParams(dimension_semantics=("arbitrary",)),
        name="final_norm",
    )(x2, g)


def _prefix_kernel(x_ref, sel_ref, o_ref):
    o_ref[...] = _dot3(x_ref[...], sel_ref[...])


def _prefix(x, sel, *, tm):
    n, k = x.shape
    w = sel.shape[1]
    assert n % tm == 0
    return pl.pallas_call(
        _prefix_kernel,
        grid=(n // tm,),
        in_specs=[pl.BlockSpec((tm, k), lambda i: (i, 0)), pl.BlockSpec((k, w), lambda i: (0, 0))],
        out_specs=pl.BlockSpec((tm, w), lambda i: (i, 0)),
        out_shape=jax.ShapeDtypeStruct((n, w), F32),
        compiler_params=pltpu.CompilerParams(dimension_semantics=("arbitrary",)),
        name="logf_prefix",
    )(x, sel)


def _block_diag_queries(q, n_rep, row_shift, lane_shift):
    qt = jnp.concatenate([q] * n_rep, axis=0)
    rg = lax.broadcasted_iota(jnp.int32, qt.shape, 0) >> row_shift
    lg = lax.broadcasted_iota(jnp.int32, qt.shape, 1) >> lane_shift
    return jnp.where(rg == lg, qt, 0.0).astype(BF16)


def _pad_rows(x, rows):
    return jnp.concatenate([x, jnp.zeros((rows - x.shape[0], x.shape[1]), x.dtype)], axis=0)


def _expand_heads(r, n_heads, n_dec):
    sub = lax.broadcasted_iota(jnp.int32, (2 * n_dec, LANES), 0)
    pieces = []
    for p in range(n_heads // 2):
        a = jnp.broadcast_to(r[:, (2 * p) * LANES:(2 * p + 1) * LANES], (2 * n_dec, LANES))
        b = jnp.broadcast_to(r[:, (2 * p + 1) * LANES:(2 * p + 2) * LANES], (2 * n_dec, LANES))
        pieces.append(jnp.where(sub < n_dec, a, b))
    return jnp.concatenate(pieces, axis=0)


def _sample_a_kernel(pt_ref, q_ref, kp_ref, vp_ref, kn_ref, vn_ref, o_ref, qbd_ref, c_ref, acc_ref, *, n_dec):
    del pt_ref
    j = pl.program_id(1)
    page = kp_ref.shape[2]
    n_rows = HA * n_dec
    rowi = lax.broadcasted_iota(jnp.int32, (n_rows, page), 0)
    lane = lax.broadcasted_iota(jnp.int32, (n_rows, page), 1)
    later = (lax.broadcasted_iota(jnp.int32, (page, page), 0)
             > lax.broadcasted_iota(jnp.int32, (page, page), 1)).astype(BF16)

    def tile(kt, vt, mask):
        z = _dot_nt(qbd_ref[...], kt)
        ls = _log_sigmoid(z)
        lk = ls - z
        if mask is not None:
            lk = jnp.where(mask, lk, 0.0)
        hi = lk.astype(BF16)
        lo = (lk - hi.astype(F32)).astype(BF16)
        w = jnp.exp(ls + _dot(hi, later) + _dot(lo, later) + c_ref[...])
        if mask is not None:
            w = jnp.where(mask, w, 0.0)
        acc_ref[...] += _dot(w.astype(BF16), vt)
        c_ref[...] += jnp.sum(lk, axis=1, keepdims=True)

    @pl.when(j == 0)
    def _():
        qbd_ref[...] = _block_diag_queries(q_ref[0][:, 0:WA], HA, 2, 6)
        c_ref[...] = jnp.zeros_like(c_ref)
        acc_ref[...] = jnp.zeros_like(acc_ref)
        strict = lane < (rowi & (n_dec - 1))
        tile(_pad_rows(kn_ref[0], page).astype(BF16), _pad_rows(vn_ref[0], page).astype(BF16), strict)

    tile(kp_ref[0, 0].astype(BF16), vp_ref[0, 0].astype(BF16), None)

    @pl.when(j == pl.num_programs(1) - 1)
    def _():
        laneh = lax.broadcasted_iota(jnp.int32, (n_dec, WA), 1) >> 6
        out = jnp.zeros((n_dec, WA), F32)
        for h in range(HA):
            out = jnp.where(laneh == h, acc_ref[h * n_dec:(h + 1) * n_dec, :], out)
        o_ref[0] = out


def _softmax_tile(qbd_ref, m_ref, l_ref, acc_ref, kt, vt, bias, mask):
    s = _dot_nt(qbd_ref[...], kt)
    if bias is not None:
        s = s + bias
    if mask is not None:
        s = jnp.where(mask, s, NEG_INF)
    m = m_ref[...]
    m_new = jnp.maximum(m, jnp.max(s, axis=1, keepdims=True))
    alpha = jnp.exp(m - m_new)
    p = jnp.exp(s - m_new)
    l_ref[...] = alpha * l_ref[...] + jnp.sum(p, axis=1, keepdims=True)
    acc_ref[...] = alpha * acc_ref[...] + _dot(p.astype(BF16), vt)
    m_ref[...] = m_new


def _sample_b_kernel(pt_ref, q_ref, kp_ref, vp_ref, fp_ref, kn_ref, vn_ref, fn_ref, o_ref,
                     qbd_ref, m_ref, l_ref, acc_ref, cs_ref, *, n_dec):
    del pt_ref
    j = pl.program_id(1)
    page = kp_ref.shape[2]
    n_rows = HB * n_dec
    rowi = lax.broadcasted_iota(jnp.int32, (n_rows, page), 0)
    lane = lax.broadcasted_iota(jnp.int32, (n_rows, page), 1)
    state = (qbd_ref, m_ref, l_ref, acc_ref)

    @pl.when(j == 0)
    def _():
        qbd_ref[...] = _block_diag_queries(q_ref[0][:, WA:WA + WB], HB, 2, 6)
        m_ref[...] = jnp.full_like(m_ref, NEG_INF)
        l_ref[...] = jnp.zeros_like(l_ref)
        acc_ref[...] = jnp.zeros_like(acc_ref)
        fnew = _expand_heads(fn_ref[0], HB, n_dec)
        qpos = rowi & (n_dec - 1)
        cs0 = jnp.sum(jnp.where(lane == qpos, fnew, 0.0), axis=1, keepdims=True)
        _softmax_tile(*state, _pad_rows(kn_ref[0], page).astype(BF16), _pad_rows(vn_ref[0], page).astype(BF16),
                      cs0 - fnew, lane <= qpos)
        cs_ref[...] = cs0

    fpg = _expand_heads(fp_ref[0, 0], HB, n_dec)
    cs = cs_ref[...] + fpg[:, page - 1:page]
    _softmax_tile(*state, kp_ref[0, 0].astype(BF16), vp_ref[0, 0].astype(BF16), cs - fpg, None)
    cs_ref[...] = cs

    @pl.when(j == pl.num_programs(1) - 1)
    def _():
        laneh = lax.broadcasted_iota(jnp.int32, (n_dec, WB), 1) >> 6
        out = jnp.zeros((n_dec, WB), F32)
        for h in range(HB):
            rows = slice(h * n_dec, (h + 1) * n_dec)
            out = jnp.where(laneh == h, acc_ref[rows, :] / l_ref[rows, :], out)
        o_ref[0] = out


def _sample_c_kernel(pt_ref, q_ref, kp_ref, vp_ref, kn_ref, vn_ref, dl_ref, dn_ref, o_ref,
                     qbd_ref, m_ref, l_ref, acc_ref, *, n_dec, lam_init):
    del pt_ref
    j = pl.program_id(1)
    page = kp_ref.shape[2]
    n_rows = HC * 2 * n_dec
    rowi = lax.broadcasted_iota(jnp.int32, (n_rows, page), 0)
    lane = lax.broadcasted_iota(jnp.int32, (n_rows, page), 1)
    state = (qbd_ref, m_ref, l_ref, acc_ref)

    @pl.when(j == 0)
    def _():
        qbd_ref[...] = _block_diag_queries(q_ref[0][:, WA + WB:], HC * 2, 2, 5)
        m_ref[...] = jnp.full_like(m_ref, NEG_INF)
        l_ref[...] = jnp.zeros_like(l_ref)
        acc_ref[...] = jnp.zeros_like(acc_ref)
        _softmax_tile(*state, _pad_rows(kn_ref[0], page).astype(BF16), _pad_rows(vn_ref[0], page).astype(BF16),
                      None, lane <= (rowi & (n_dec - 1)))

    _softmax_tile(*state, kp_ref[0, 0].astype(BF16), vp_ref[0, 0].astype(BF16), None, None)

    @pl.when(j == pl.num_programs(1) - 1)
    def _():
        lam = _diff_lambda(dl_ref, lam_init)
        laneh = lax.broadcasted_iota(jnp.int32, (n_dec, WC), 1) >> 6
        out = jnp.zeros((n_dec, WC), F32)
        for h in range(HC):
            r1 = slice((2 * h) * n_dec, (2 * h + 1) * n_dec)
            r2 = slice((2 * h + 1) * n_dec, (2 * h + 2) * n_dec)
            o = acc_ref[r1, :] / l_ref[r1, :] - lam * (acc_ref[r2, :] / l_ref[r2, :])
            mine = laneh == h
            ms = jnp.sum(jnp.where(mine, o * o, 0.0), axis=1, keepdims=True) * (1.0 / HEAD_DIM)
            out = jnp.where(mine, o * lax.rsqrt(ms + EPS), out)
        o_ref[0] = out * dn_ref[...] * (1.0 - lam_init)


def _sample_attn(kind, page_table, q3, k_cache, v_cache, k_new, v_new, extra_in, layer, *, lam_init=None):
    nb, n_dec, d = q3.shape
    assert 2 * n_dec == 8, "two heads' query rows must fill one 8-sublane tile"
    n_pages = page_table.shape[1]
    page, w = k_cache.shape[2], k_cache.shape[3]
    pg = lambda b, j, pt: (layer, pt[b, n_pages - 1 - j], 0, 0)
    per_b = lambda b, j, pt: (b, 0, 0)
    const = lambda b, j, pt: (0, 0)
    q_spec = pl.BlockSpec((1, n_dec, d), per_b)
    page_spec = pl.BlockSpec((1, 1, page, w), pg)
    new_spec = pl.BlockSpec((1, n_dec, w), per_b)
    if kind == "a":
        n_rows = HA * n_dec
        body = functools.partial(_sample_a_kernel, n_dec=n_dec)
        in_specs = [q_spec, page_spec, page_spec, new_spec, new_spec]
        args = (q3, k_cache, v_cache, k_new, v_new)
        scratch = [pltpu.VMEM((n_rows, w), BF16), pltpu.VMEM((n_rows, 1), F32), pltpu.VMEM((n_rows, w), F32)]
    elif kind == "b":
        n_rows = HB * n_dec
        fpage, fnew = extra_in
        body = functools.partial(_sample_b_kernel, n_dec=n_dec)
        in_specs = [q_spec, page_spec, page_spec, pl.BlockSpec((1, 1, 1, HB * LANES), pg),
                    new_spec, new_spec, pl.BlockSpec((1, 1, HB * LANES), per_b)]
        args = (q3, k_cache, v_cache, fpage, k_new, v_new, fnew)
        scratch = [pltpu.VMEM((n_rows, w), BF16), pltpu.VMEM((n_rows, 1), F32), pltpu.VMEM((n_rows, 1), F32),
                   pltpu.VMEM((n_rows, w), F32), pltpu.VMEM((n_rows, 1), F32)]
    else:
        n_rows = HC * 2 * n_dec
        dl, dn = extra_in
        body = functools.partial(_sample_c_kernel, n_dec=n_dec, lam_init=lam_init)
        in_specs = [q_spec, page_spec, page_spec, new_spec, new_spec,
                    pl.BlockSpec((4, QK_C), const), pl.BlockSpec((1, w), const)]
        args = (q3, k_cache, v_cache, k_new, v_new, dl, dn)
        scratch = [pltpu.VMEM((n_rows, w), BF16), pltpu.VMEM((n_rows, 1), F32), pltpu.VMEM((n_rows, 1), F32),
                   pltpu.VMEM((n_rows, w), F32)]
    grid_spec = pltpu.PrefetchScalarGridSpec(
        num_scalar_prefetch=1,
        grid=(nb, n_pages),
        in_specs=in_specs,
        out_specs=pl.BlockSpec((1, n_dec, w), per_b),
        scratch_shapes=scratch,
    )
    return pl.pallas_call(
        body,
        grid_spec=grid_spec,
        out_shape=jax.ShapeDtypeStruct((nb, n_dec, w), F32),
        compiler_params=pltpu.CompilerParams(dimension_semantics=("arbitrary", "arbitrary"),
                                             vmem_limit_bytes=VMEM_LIMIT),
        name="sample_attn_" + kind,
    )(page_table, *args)


def _rotate_half_columns(w):
    lead = w.shape[:-1]
    g = w.reshape(lead + (w.shape[-1] // QK_C, 2, QK_C // 2))
    return jnp.concatenate([-g[..., 1:2, :], g[..., 0:1, :]], axis=-2).reshape(w.shape)


def _arrange_w_in(w_in):
    sizes = [WA] * 3 + [WB] * 3 + [HB] + [WC] * 3
    bounds = [0]
    for s in sizes:
        bounds.append(bounds[-1] + s)
    qa, ka, va, qb, kb, vb, fb, qc, kc, vc = [w_in[..., bounds[i]:bounds[i + 1]] for i in range(len(sizes))]
    pad = jnp.zeros(w_in.shape[:-1] + (LANES - HB,), w_in.dtype)
    cols = [qa, ka, va, qb, kb, vb, qc, kc, vc, _rotate_half_columns(qc), _rotate_half_columns(kc), fb, pad]
    return jnp.concatenate(cols, axis=-1).astype(BF16)


def _rope_tables(pos):
    half = QK_C // 2
    inv = ROPE_THETA ** (-jnp.arange(half, dtype=F32) / half)
    ang = pos.astype(F32)[:, None] * inv[None, :]
    reps = WC // half
    return jnp.tile(jnp.cos(ang), (1, reps)), jnp.tile(jnp.sin(ang), (1, reps))


def _pool_prefix_matrix(page, n_heads):
    r = jnp.arange(page * n_heads)
    c = jnp.arange(n_heads * LANES)
    m = ((r % n_heads)[:, None] == (c // LANES)[None, :]) & ((r // n_heads)[:, None] <= (c % LANES)[None, :])
    return m.astype(BF16)


def _new_prefix_matrix(n_dec, n_heads):
    r = jnp.arange(n_dec * LANES)
    c = jnp.arange(n_heads * LANES)
    m = (((r % LANES)[:, None] == (c // LANES)[None, :]) & ((r // LANES)[:, None] <= (c % LANES)[None, :])
         & ((c % LANES)[None, :] < n_dec))
    return m.astype(BF16)


def kernel(x_prompt, x_sample, cache_a_k, cache_a_v, cache_b_k, cache_b_v, cache_b_logf, cache_c_k, cache_c_v,
           page_table, w_in, b_forget, w_out, g_attn, diff_lambda, diff_norm, g_ffn, w_group, b_group,
           w_expert_router, b_expert_router, w_up_gate, w_up, w_down, g_final):
    nb, nt, d = x_prompt.shape
    sb, n_dec, _ = x_sample.shape
    depth = w_in.shape[0]
    n_pool, page = cache_a_k.shape[1], cache_a_k.shape[2]
    n_pages = page_table.shape[1]
    assert page == LANES
    tm = min(512, nb * nt)
    tq = min(256, nt)

    w_arr = _arrange_w_in(w_in)
    bf_pad = jnp.pad(b_forget.astype(F32), ((0, 0), (0, LANES - HB)))[:, None, :]
    wo = w_out.astype(BF16)
    wr_t = jnp.zeros((depth, _R_ROWS, d), F32)
    wr_t = wr_t.at[:, 0:N_GROUPS].set(jnp.swapaxes(w_group, 1, 2).astype(F32))
    wr_t = wr_t.at[:, _R_EXP0:_R_EXP0 + N_GROUPS * EXPERTS_PER_GROUP].set(
        jnp.swapaxes(w_expert_router, 1, 2).astype(F32))
    br = jnp.zeros((depth, _R_ROWS), F32)
    br = br.at[:, 0:N_GROUPS].set(b_group.astype(F32))
    br = br.at[:, _R_EXP0:_R_EXP0 + N_GROUPS * EXPERTS_PER_GROUP].set(b_expert_router.astype(F32))[:, :, None]
    w1 = w_up_gate.astype(BF16)
    w3 = w_up.astype(BF16)
    w2 = w_down.astype(BF16)
    dn2 = jnp.tile(diff_norm.astype(F32), (1, HEADS_PER_BLOCK))[:, None, :]
    dn_c = jnp.tile(diff_norm.astype(F32), (1, HC))[:, None, :]
    dl = diff_lambda.astype(F32)

    cos_p, sin_p = _rope_tables(jnp.arange(nt))
    pos_s = n_pages * page + jnp.arange(n_dec)
    cos_s, sin_s = _rope_tables(jnp.tile(pos_s, sb))

    ca_k = cache_a_k.reshape(depth, n_pool, page, WA)
    ca_v = cache_a_v.reshape(depth, n_pool, page, WA)
    cb_k = cache_b_k.reshape(depth, n_pool, page, WB)
    cb_v = cache_b_v.reshape(depth, n_pool, page, WB)
    cc_k = cache_c_k.reshape(depth, n_pool, page, WC)
    cc_v = cache_c_v.reshape(depth, n_pool, page, WC)
    fpage = _prefix(cache_b_logf.astype(F32).reshape(depth * n_pool, page * HB), _pool_prefix_matrix(page, HB),
                    tm=math.gcd(depth * n_pool, 256)).reshape(depth, n_pool, 1, HB * LANES)
    sel_new = _new_prefix_matrix(n_dec, HB)

    xp = x_prompt.reshape(nb * nt, d).astype(F32)
    xs = x_sample.reshape(sb * n_dec, d).astype(F32)
    ns = sb * n_dec
    rows_p = [[] for _ in range(7)]
    rows_s = [[] for _ in range(7)]
    for l in range(depth):
        lam_init = 0.8 - 0.6 * math.exp(-0.3 * l)
        g_a = g_attn[l].astype(F32)[None, :]
        g_f = g_ffn[l].astype(F32)[None, :]

        q, ka, va, kb, vb, lf, lfp, kc, vc = _inproj(xp, g_a, w_arr[l], bf_pad[l], cos_p, sin_p,
                                                     tm=tm, n_pos_tiles=nt // tm)
        b3 = lambda t: t.reshape(nb, nt, t.shape[-1])
        q3 = b3(q)
        oa = _prompt_attn("a", q3, b3(ka), b3(va), (), tq=tq)
        ob = _prompt_attn("b", q3, b3(kb), b3(vb), (b3(lfp),), tq=tq)
        oc = _prompt_attn("c", q3, b3(kc), b3(vc), (dl[l], dn2[l]), tq=tq, lam_init=lam_init)
        flat = lambda t: t.reshape(nb * nt, t.shape[-1])
        xp = _outproj_moe(xp, flat(oa), flat(ob), flat(oc), wo[l], g_f, wr_t[l], br[l], w1, w3, w2, l, tm=tm)
        for i, t in enumerate((ka, va, kb, vb, lf, kc, vc)):
            rows_p[i].append(t)

        q, ka, va, kb, vb, lf, lfp, kc, vc = _inproj(xs, g_a, w_arr[l], bf_pad[l], cos_s, sin_s,
                                                     tm=ns, n_pos_tiles=1)
        s3 = lambda t: t.reshape(sb, n_dec, t.shape[-1])
        q3 = s3(q.astype(F32))
        fnew = _prefix(lfp.reshape(sb, n_dec * LANES), sel_new, tm=sb).reshape(sb, 1, HB * LANES)
        oa = _sample_attn("a", page_table, q3, ca_k, ca_v, s3(ka), s3(va), (), l)
        ob = _sample_attn("b", page_table, q3, cb_k, cb_v, s3(kb), s3(vb), (fpage, fnew), l)
        oc = _sample_attn("c", page_table, q3, cc_k, cc_v, s3(kc), s3(vc), (dl[l], dn_c[l]), l, lam_init=lam_init)
        sflat = lambda t: t.reshape(ns, t.shape[-1]).astype(BF16)
        xs = _outproj_moe(xs, sflat(oa), sflat(ob), sflat(oc), wo[l], g_f, wr_t[l], br[l], w1, w3, w2, l, tm=ns)
        for i, t in enumerate((ka, va, kb, vb, lf, kc, vc)):
            rows_s[i].append(t)

    g_fin = g_final.astype(F32)[None, :]
    y_prompt = _final_norm(xp, g_fin, tm=tm).reshape(nb, nt, d)
    y_sample = _final_norm(xs, g_fin, tm=ns).reshape(sb, n_dec, d)

    def stack(rows, b, t):
        outs = []
        for i, r in enumerate(rows):
            a = jnp.stack(r, axis=0)
            if i == 4:
                outs.append(a.reshape(depth, b, t, HB))
            else:
                outs.append(a.reshape(depth, b, t, a.shape[-1] // HEAD_DIM, HEAD_DIM))
        return outs

    return (y_prompt, y_sample, *stack(rows_p, nb, nt), *stack(rows_s, sb, n_dec))
```

```python
import functools
import math

import jax
import jax.numpy as jnp
from jax import lax
from jax.experimental import pallas as pl
from jax.experimental.pallas import tpu as pltpu

F32 = jnp.float32
BF16 = jnp.bfloat16

HEAD_DIM = 64
HA, HB, HC = 4, 6, 6
QK_C = HEAD_DIM // 2
WA, WB, WC = HA * HEAD_DIM, HB * HEAD_DIM, HC * HEAD_DIM
ROPE_THETA = 10000.0
N_GROUPS = 4
EXPERTS_PER_GROUP = 4
EPS = 1e-6
LOG2E = 1.4426950408889634
LANES = 128
SUBLANES = 8
HEADS_PER_BLOCK = LANES // HEAD_DIM
VMEM_LIMIT = 56 * 1024 * 1024
NEG_INF = float("-inf")

_C_QA, _C_KA, _C_VA = 0, WA, 2 * WA
_C_QB = 3 * WA
_C_KB, _C_VB = _C_QB + WB, _C_QB + 2 * WB
_C_QC = _C_QB + 3 * WB
_C_KC, _C_VC = _C_QC + WC, _C_QC + 2 * WC
_C_QCR, _C_KCR = _C_QC + 3 * WC, _C_QC + 4 * WC
_C_F = _C_QC + 5 * WC
_P_ARR = _C_F + LANES


def _dot(a, b):
    return jnp.dot(a, b, preferred_element_type=F32)


def _dot_nt(a, b):
    return lax.dot_general(a, b, (((1,), (1,)), ((), ())), preferred_element_type=F32)


def _log_sigmoid(z):
    return jnp.minimum(z, 0.0) - jnp.log(1.0 + jnp.exp(-jnp.abs(z)))


def _log2_sigmoid(z2):
    return jnp.minimum(z2, 0.0) - jnp.log2(1.0 + jnp.exp2(-jnp.abs(z2)))


def _split3(x):
    hi = x.astype(BF16)
    r = x - hi.astype(F32)
    mid = r.astype(BF16)
    lo = (r - mid.astype(F32)).astype(BF16)
    return hi, mid, lo


def _dot3(x, m):
    hi, mid, lo = _split3(x)
    return _dot(hi, m) + _dot(mid, m) + _dot(lo, m)


def _cumsum_rows(x, incl):
    hi, mid, lo = _split3(x)
    return _dot(incl, hi) + _dot(incl, mid) + _dot(incl, lo)


def _inproj_common(x_ref, g_ref, w_ref, bf_ref, cos_ref, sin_ref, q_ref, ka_ref, kb_ref, lf_ref, lfp_ref, kc_ref):
    x = x_ref[...]
    ms = jnp.mean(x * x, axis=-1, keepdims=True)
    h = (x * lax.rsqrt(ms + EPS) * g_ref[...]).astype(BF16)

    def proj(c0, width):
        return _dot(h, w_ref[:, c0:c0 + width])

    q_ref[:, 0:WA] = (proj(_C_QA, WA) * (HEAD_DIM ** -0.5 * LOG2E)).astype(BF16)
    ka_ref[...] = proj(_C_KA, WA)
    q_ref[:, WA:WA + WB] = (proj(_C_QB, WB) * (HEAD_DIM ** -0.5 * LOG2E)).astype(BF16)
    kb_ref[...] = proj(_C_KB, WB)
    cos = cos_ref[...]
    sin = sin_ref[...]
    qc = proj(_C_QC, WC) * cos + proj(_C_QCR, WC) * sin
    q_ref[:, WA + WB:] = (qc * (QK_C ** -0.5 * LOG2E)).astype(BF16)
    kc_ref[...] = proj(_C_KC, WC) * cos + proj(_C_KCR, WC) * sin
    lf = _log_sigmoid(proj(_C_F, LANES) + bf_ref[...])
    lane = lax.broadcasted_iota(jnp.int32, lf.shape, 1)
    lfp_ref[...] = jnp.where(lane < HB, lf, 0.0)
    lf_ref[...] = lf[:, :HB]
    return h, proj


def _inproj_rows_kernel(x_ref, g_ref, w_ref, bf_ref, cos_ref, sin_ref,
                        q_ref, ka_ref, va_ref, kb_ref, vb_ref, lf_ref, lfp_ref, kc_ref, vc_ref):
    _, proj = _inproj_common(x_ref, g_ref, w_ref, bf_ref, cos_ref, sin_ref,
                             q_ref, ka_ref, kb_ref, lf_ref, lfp_ref, kc_ref)
    va_ref[...] = proj(_C_VA, WA)
    vb_ref[...] = proj(_C_VB, WB)
    vc_ref[...] = proj(_C_VC, WC)


def _inproj_vt_kernel(x_ref, g_ref, w_ref, wvt_ref, bf_ref, cos_ref, sin_ref,
                      q_ref, ka_ref, vat_ref, kb_ref, vbt_ref, lf_ref, lfp_ref, kc_ref, vct_ref):
    h, _ = _inproj_common(x_ref, g_ref, w_ref, bf_ref, cos_ref, sin_ref,
                          q_ref, ka_ref, kb_ref, lf_ref, lfp_ref, kc_ref)
    vat_ref[0] = _dot_nt(wvt_ref[0:WA, :], h)
    vbt_ref[0] = _dot_nt(wvt_ref[WA:WA + WB, :], h)
    vct_ref[0] = _dot_nt(wvt_ref[WA + WB:, :], h)


def _inproj(x2, g, w_arr, w_vt, bf_pad, cos_t, sin_t, *, tm, n_pos_tiles, n_batch=None):
    n, d = x2.shape
    assert n % tm == 0
    grid = (n // tm,)
    row = lambda i: (i, 0)
    const = lambda i: (0, 0)
    pos = lambda i: (i % n_pos_tiles, 0)
    rows = lambda w, dt=F32: (jax.ShapeDtypeStruct((n, w), dt), pl.BlockSpec((tm, w), row))
    in_specs = [pl.BlockSpec((tm, d), row), pl.BlockSpec((1, d), const), pl.BlockSpec((d, _P_ARR), const)]
    args = [x2, g, w_arr]
    if w_vt is None:
        body = _inproj_rows_kernel
        val = rows
    else:
        body = _inproj_vt_kernel
        in_specs.append(pl.BlockSpec(w_vt.shape, const))
        args.append(w_vt)
        nt = n // n_batch
        tmap = lambda i: (i // n_pos_tiles, 0, i % n_pos_tiles)
        val = lambda w: (jax.ShapeDtypeStruct((n_batch, w, nt), F32), pl.BlockSpec((1, w, tm), tmap))
    in_specs += [pl.BlockSpec((1, LANES), const), pl.BlockSpec((tm, WC), pos), pl.BlockSpec((tm, WC), pos)]
    args += [bf_pad, cos_t, sin_t]
    outs = [rows(WA + WB + WC, BF16), rows(WA), val(WA), rows(WB), val(WB), rows(HB), rows(LANES), rows(WC), val(WC)]
    return pl.pallas_call(
        body,
        grid=grid,
        in_specs=in_specs,
        out_specs=tuple(o[1] for o in outs),
        out_shape=tuple(o[0] for o in outs),
        compiler_params=pltpu.CompilerParams(dimension_semantics=("arbitrary",), vmem_limit_bytes=VMEM_LIMIT),
        name="inproj",
    )(*args)


def _tile_iotas(t):
    return lax.broadcasted_iota(jnp.int32, (t, t), 0), lax.broadcasted_iota(jnp.int32, (t, t), 1)


def _masked_queries(q, width):
    lane = lax.broadcasted_iota(jnp.int32, q.shape, 1)
    zero = jnp.zeros_like(q)
    return jnp.concatenate([jnp.where((lane >= g * width) & (lane < (g + 1) * width), q, zero)
                            for g in range(LANES // width)], axis=0)


def _cast_kv(qi, k_ref, vt_ref, kb_ref, vtb_ref):
    @pl.when(qi == 0)
    def _():
        kb_ref[...] = k_ref[0].astype(BF16)
        vtb_ref[...] = vt_ref[0].astype(BF16)


def _sweep(qi, tq, kb_ref, qs, s_ref, update, init):
    def scores(kj):
        kt = kb_ref[pl.ds(pl.multiple_of(kj * tq, tq), tq), :]
        half = qs.shape[0] // 2
        return jnp.concatenate([_dot_nt(kt, qs[:half]), _dot_nt(kt, qs[half:])], axis=1)

    nothing = lambda: None
    state = update([qi], [scores(qi)], init, True, nothing)
    s_ref[...] = scores(jnp.maximum(qi - 1, 0))

    def body(it, st):
        kj = qi - 1 - 2 * it
        s_a = s_ref[...]
        s_b = scores(kj - 1)

        def prefetch():
            s_ref[...] = scores(jnp.maximum(kj - 2, 0))

        return update([kj, kj - 1], [s_a, s_b], st, False, prefetch)

    state = lax.fori_loop(0, qi // 2, body, state)
    return lax.cond(qi % 2 == 1, lambda st: update([0], [scores(0)], st, False, nothing), lambda st: st, state)


def _attn_a_kernel(q_ref, k_ref, vt_ref, o_ref, kb_ref, vtb_ref, s_ref, *, tq):
    qi = pl.program_id(2)
    _cast_kv(qi, k_ref, vt_ref, kb_ref, vtb_ref)
    qs = _masked_queries(q_ref[0], HEAD_DIM)
    key, qry = _tile_iotas(tq)
    strict = key < qry
    later = (qry > key).astype(BF16)

    def tiles(kjs, z_alls, carry, masked, prefetch):
        n_g = len(kjs) * HEADS_PER_BLOCK
        zs = [z_all[:, hh * tq:(hh + 1) * tq] for z_all in z_alls for hh in range(HEADS_PER_BLOCK)]
        ls = [_log2_sigmoid(z) for z in zs]
        lk = [l - z for l, z in zip(ls, zs)]
        if masked:
            lk = [jnp.where(strict, t, 0.0) for t in lk]
        prefetch()
        cum = _dot(later, jnp.concatenate([t.astype(BF16) for t in lk], axis=1))
        ws = [jnp.exp2(ls[g] + cum[:, g * tq:(g + 1) * tq]) for g in range(n_g)]
        if masked:
            ws = [jnp.where(strict, w, 0.0) for w in ws]
        new = list(carry)
        for t, kj in enumerate(kjs):
            vt = vtb_ref[:, pl.ds(pl.multiple_of(kj * tq, tq), tq)]
            for hh in range(HEADS_PER_BLOCK):
                g = t * HEADS_PER_BLOCK + hh
                c, acc = new[hh]
                acc = acc + _dot(vt[hh * HEAD_DIM:(hh + 1) * HEAD_DIM, :], ws[g].astype(BF16)) * jnp.exp2(c)
                new[hh] = (c + jnp.sum(lk[g], axis=0, keepdims=True), acc)
        return tuple(new)

    init = tuple((jnp.zeros((1, tq), F32), jnp.zeros((HEAD_DIM, tq), F32)) for _ in range(HEADS_PER_BLOCK))
    carry = _sweep(qi, tq, kb_ref, qs, s_ref, tiles, init)
    o_t = jnp.concatenate([carry[hh][1] for hh in range(HEADS_PER_BLOCK)], axis=0)
    o_ref[0] = o_t.T.astype(BF16)


def _softmax_step(s, state, vt_rows, masked, causal):
    m, l, acc = state
    if masked:
        s = jnp.where(causal, s, NEG_INF)
    m_new = jnp.maximum(m, jnp.max(s, axis=0, keepdims=True))
    alpha = jnp.exp2(m - m_new)
    p = jnp.exp2(s - m_new)
    l = alpha * l + jnp.sum(p, axis=0, keepdims=True)
    acc = alpha * acc + _dot(vt_rows, p.astype(BF16))
    return m_new, l, acc


def _softmax_init(tq):
    return (jnp.full((1, tq), NEG_INF, F32), jnp.zeros((1, tq), F32), jnp.zeros((HEAD_DIM, tq), F32))


def _attn_b_kernel(q_ref, k_ref, vt_ref, lf_ref, o_ref, kb_ref, vtb_ref, s_ref, fcol_ref, frow_ref, *, tq, n_t):
    pair = pl.program_id(1)
    qi = pl.program_id(2)
    _cast_kv(qi, k_ref, vt_ref, kb_ref, vtb_ref)
    key, qry = _tile_iotas(tq)

    @pl.when(qi == 0)
    def _():
        incl = (qry <= key).astype(BF16)
        for hh in range(HEADS_PER_BLOCK):
            hg = pair * HEADS_PER_BLOCK + hh
            carry = jnp.zeros((1, LANES), F32)
            for c in range(n_t // tq):
                lf = lf_ref[0, c * tq:(c + 1) * tq, :]
                lane = lax.broadcasted_iota(jnp.int32, lf.shape, 1)
                colv = jnp.sum(jnp.where(lane == hg, lf, 0.0), axis=1, keepdims=True)
                f = _cumsum_rows(jnp.broadcast_to(colv, (tq, LANES)), incl) + carry
                carry = f[tq - 1:tq, :]
                f2 = f * LOG2E
                fcol_ref[hh, c * tq:(c + 1) * tq, :] = f2
                frow_ref[hh, :, c * tq:(c + 1) * tq] = f2.T[0:SUBLANES, :]

    qs = _masked_queries(q_ref[0], HEAD_DIM)
    causal = key <= qry
    qoff = pl.multiple_of(qi * tq, tq)
    fqs = [frow_ref[hh, 0:1, pl.ds(qoff, tq)] for hh in range(HEADS_PER_BLOCK)]

    def tiles(kjs, z_alls, carry, masked, prefetch):
        for t, (kj, z_all) in enumerate(zip(kjs, z_alls)):
            if t == 1:
                prefetch()
            off = pl.multiple_of(kj * tq, tq)
            vt = vtb_ref[:, pl.ds(off, tq)]
            new = []
            for hh in range(HEADS_PER_BLOCK):
                fk = fcol_ref[hh, pl.ds(off, tq), :]
                fk = jnp.concatenate([fk] * (tq // LANES), axis=1)
                s = z_all[:, hh * tq:(hh + 1) * tq] + fqs[hh] - fk
                new.append(_softmax_step(s, carry[hh], vt[hh * HEAD_DIM:(hh + 1) * HEAD_DIM, :], masked, causal))
            carry = tuple(new)
        return carry

    carry = _sweep(qi, tq, kb_ref, qs, s_ref, tiles, tuple(_softmax_init(tq) for _ in range(HEADS_PER_BLOCK)))
    o_t = jnp.concatenate([carry[hh][2] / carry[hh][1] for hh in range(HEADS_PER_BLOCK)], axis=0)
    o_ref[0] = o_t.T.astype(BF16)


def _diff_lambda(dl_ref, lam_init):
    dl = dl_ref[...]
    a = jnp.sum(dl[0:1, :] * dl[1:2, :], axis=1, keepdims=True)
    b = jnp.sum(dl[2:3, :] * dl[3:4, :], axis=1, keepdims=True)
    return jnp.exp(a) - jnp.exp(b) + lam_init


def _attn_c_kernel(q_ref, k_ref, vt_ref, dl_ref, dn_ref, o_ref, kb_ref, vtb_ref, s_ref, *, tq, lam_init):
    qi = pl.program_id(2)
    _cast_kv(qi, k_ref, vt_ref, kb_ref, vtb_ref)
    qs = _masked_queries(q_ref[0], QK_C)
    n_maps = LANES // QK_C
    key, qry = _tile_iotas(tq)
    causal = key <= qry

    def tiles(kjs, s_alls, carry, masked, prefetch):
        for t, (kj, s_all) in enumerate(zip(kjs, s_alls)):
            if t == 1:
                prefetch()
            vt = vtb_ref[:, pl.ds(pl.multiple_of(kj * tq, tq), tq)]
            new = []
            for mp in range(n_maps):
                hh = mp // 2
                s = s_all[:, mp * tq:(mp + 1) * tq]
                new.append(_softmax_step(s, carry[mp], vt[hh * HEAD_DIM:(hh + 1) * HEAD_DIM, :], masked, causal))
            carry = tuple(new)
        return carry

    carry = _sweep(qi, tq, kb_ref, qs, s_ref, tiles, tuple(_softmax_init(tq) for _ in range(n_maps)))
    lam = _diff_lambda(dl_ref, lam_init)
    outs = []
    for hh in range(HEADS_PER_BLOCK):
        m1, m2 = carry[2 * hh], carry[2 * hh + 1]
        o = m1[2] / m1[1] - lam * (m2[2] / m2[1])
        ms = jnp.mean(o * o, axis=0, keepdims=True)
        outs.append(o * lax.rsqrt(ms + EPS) * dn_ref[...] * (1.0 - lam_init))
    o_ref[0] = jnp.concatenate(outs, axis=0).T.astype(BF16)


def _prompt_attn(kind, q_all, k, vt, extra, *, tq, lam_init=None):
    nb, nt, w = k.shape
    assert nt % tq == 0 and tq % LANES == 0
    n_pairs = w // LANES
    q_blk0 = {"a": 0, "b": WA // LANES, "c": (WA + WB) // LANES}[kind]
    grid = (nb, n_pairs, nt // tq)
    q_spec = pl.BlockSpec((1, tq, LANES), lambda b, p, i: (b, i, q_blk0 + p))
    k_spec = pl.BlockSpec((1, nt, LANES), lambda b, p, i: (b, 0, p))
    vt_spec = pl.BlockSpec((1, LANES, nt), lambda b, p, i: (b, p, 0))
    o_spec = pl.BlockSpec((1, tq, LANES), lambda b, p, i: (b, i, p))
    n_groups = LANES // (QK_C if kind == "c" else HEAD_DIM)
    scratch = [pltpu.VMEM((nt, LANES), BF16), pltpu.VMEM((LANES, nt), BF16), pltpu.VMEM((tq, n_groups * tq), F32)]
    in_specs = [q_spec, k_spec, vt_spec]
    if kind == "a":
        body = functools.partial(_attn_a_kernel, tq=tq)
    elif kind == "b":
        body = functools.partial(_attn_b_kernel, tq=tq, n_t=nt)
        in_specs.append(pl.BlockSpec((1, nt, LANES), lambda b, p, i: (b, 0, 0)))
        scratch += [pltpu.VMEM((HEADS_PER_BLOCK, nt, LANES), F32), pltpu.VMEM((HEADS_PER_BLOCK, SUBLANES, nt), F32)]
    else:
        body = functools.partial(_attn_c_kernel, tq=tq, lam_init=lam_init)
        in_specs += [pl.BlockSpec((4, QK_C), lambda b, p, i: (0, 0)),
                     pl.BlockSpec((HEAD_DIM, 1), lambda b, p, i: (0, 0))]
    return pl.pallas_call(
        body,
        grid=grid,
        in_specs=in_specs,
        out_specs=o_spec,
        out_shape=jax.ShapeDtypeStruct((nb, nt, w), BF16),
        scratch_shapes=scratch,
        compiler_params=pltpu.CompilerParams(dimension_semantics=("arbitrary", "arbitrary", "arbitrary"),
                                             vmem_limit_bytes=VMEM_LIMIT),
        name="prompt_attn_" + kind,
    )(q_all, k, vt, *extra)


_R_EXP0 = 8
_R_ROWS = 32


def _router_rows(lt):
    g = [lt[j:j + 1, :] for j in range(N_GROUPS)]
    gmax = functools.reduce(jnp.maximum, g)
    idx = jnp.full(gmax.shape, N_GROUPS - 1, jnp.int32)
    for j in range(N_GROUPS - 2, -1, -1):
        idx = jnp.where(g[j] == gmax, j, idx)
    g_top = 1.0 / functools.reduce(jnp.add, [jnp.exp(gj - gmax) for gj in g])
    e_in = []
    for j in range(EXPERTS_PER_GROUP):
        r = _R_EXP0 + (N_GROUPS - 1) * EXPERTS_PER_GROUP + j
        ej = lt[r:r + 1, :]
        for gg in range(N_GROUPS - 2, -1, -1):
            r = _R_EXP0 + gg * EXPERTS_PER_GROUP + j
            ej = jnp.where(idx == gg, lt[r:r + 1, :], ej)
        e_in.append(ej)
    emax = functools.reduce(jnp.maximum, e_in)
    u = [jnp.exp(ej - emax) for ej in e_in]
    usum = functools.reduce(jnp.add, u)
    p = [uj / usum for uj in u]
    top1 = functools.reduce(jnp.maximum, p)
    i1 = jnp.full(top1.shape, EXPERTS_PER_GROUP - 1, jnp.int32)
    for j in range(EXPERTS_PER_GROUP - 2, -1, -1):
        i1 = jnp.where(p[j] == top1, j, i1)
    rest = [jnp.where(i1 == j, -1.0, p[j]) for j in range(EXPERTS_PER_GROUP)]
    top2 = functools.reduce(jnp.maximum, rest)
    i2 = jnp.full(top2.shape, EXPERTS_PER_GROUP - 1, jnp.int32)
    for j in range(EXPERTS_PER_GROUP - 2, -1, -1):
        i2 = jnp.where(rest[j] == top2, j, i2)
    denom = top1 + top2
    gw = [jnp.where((i1 == j) | (i2 == j), p[j] / denom * g_top, 0.0) for j in range(EXPERTS_PER_GROUP)]
    return gw, idx


def _outproj_moe_kernel(x_ref, oa_ref, ob_ref, oc_ref, wo_ref, gf_ref, wr_ref, br_ref, w1_ref, w3_ref, w2_ref,
                        out_ref, xmid_ref, h2_ref, gate_ref, y_ref):
    g = pl.program_id(1)
    tm = x_ref.shape[0]

    @pl.when(g == 0)
    def _():
        xm = (x_ref[...] + _dot(oa_ref[...], wo_ref[0:WA, :]) + _dot(ob_ref[...], wo_ref[WA:WA + WB, :])
              + _dot(oc_ref[...], wo_ref[WA + WB:, :]))
        xmid_ref[...] = xm
        ms = jnp.mean(xm * xm, axis=-1, keepdims=True)
        h2 = xm * lax.rsqrt(ms + EPS) * gf_ref[...]
        h2_ref[...] = h2.astype(BF16)
        lt = lax.dot_general(wr_ref[...], h2, (((1,), (1,)), ((), ())), precision=lax.Precision.HIGHEST,
                             preferred_element_type=F32) + br_ref[...]
        gw, idx = _router_rows(lt)
        sub = lax.broadcasted_iota(jnp.int32, (SUBLANES, tm), 0)
        rows = jnp.where(sub == EXPERTS_PER_GROUP, idx.astype(F32), 0.0)
        for j in range(EXPERTS_PER_GROUP):
            rows = jnp.where(sub == j, gw[j], rows)
        full = jnp.concatenate([rows, jnp.zeros((LANES - SUBLANES, tm), F32)], axis=0)
        gate_ref[...] = full.T
        y_ref[...] = jnp.zeros_like(y_ref)

    h2b = h2_ref[...]
    gate = gate_ref[...]
    mine = gate[:, EXPERTS_PER_GROUP:EXPERTS_PER_GROUP + 1] == g.astype(F32)
    y = y_ref[...]
    for e in range(EXPERTS_PER_GROUP):
        a = _dot(h2b, w1_ref[0, e])
        b = _dot(h2b, w3_ref[0, e])
        gcol = jnp.where(mine, gate[:, e:e + 1], 0.0)
        hid = a * (1.0 / (1.0 + jnp.exp(-a))) * b * gcol
        y = y + _dot(hid.astype(BF16), w2_ref[0, e])
    y_ref[...] = y

    @pl.when(g == N_GROUPS - 1)
    def _():
        out_ref[...] = xmid_ref[...] + y_ref[...]


def _outproj_moe(x2, oa, ob, oc, wo, gf, wr_t, br, w1, w3, w2, layer, *, tm):
    n, d = x2.shape
    assert n % tm == 0
    f = w1.shape[-1]
    grid = (n // tm, N_GROUPS)
    row = lambda i, g: (i, 0)
    const = lambda i, g: (0, 0)
    wsel = lambda i, g: (layer, g, 0, 0)
    return pl.pallas_call(
        _outproj_moe_kernel,
        grid=grid,
        in_specs=[
            pl.BlockSpec((tm, d), row),
            pl.BlockSpec((tm, WA), row), pl.BlockSpec((tm, WB), row), pl.BlockSpec((tm, WC), row),
            pl.BlockSpec((d, d), const),
            pl.BlockSpec((1, d), const),
            pl.BlockSpec((_R_ROWS, d), const),
            pl.BlockSpec((_R_ROWS, 1), const),
            pl.BlockSpec((1, EXPERTS_PER_GROUP, d, f), wsel),
            pl.BlockSpec((1, EXPERTS_PER_GROUP, d, f), wsel),
            pl.BlockSpec((1, EXPERTS_PER_GROUP, f, d), wsel),
        ],
        out_specs=pl.BlockSpec((tm, d), row),
        out_shape=jax.ShapeDtypeStruct((n, d), F32),
        scratch_shapes=[pltpu.VMEM((tm, d), F32), pltpu.VMEM((tm, d), BF16), pltpu.VMEM((tm, LANES), F32),
                        pltpu.VMEM((tm, d), F32)],
        compiler_params=pltpu.CompilerParams(dimension_semantics=("arbitrary", "arbitrary"),
                                             vmem_limit_bytes=VMEM_LIMIT),
        name="outproj_moe",
    )(x2, oa, ob, oc, wo, gf, wr_t, br, w1, w3, w2)


def _final_norm_kernel(x_ref, g_ref, o_ref):
    x = x_ref[...]
    ms = jnp.mean(x * x, axis=-1, keepdims=True)
    o_ref[...] = x * lax.rsqrt(ms + EPS) * g_ref[...]


def _final_norm(x2, g, *, tm):
    n, d = x2.shape
    assert n % tm == 0
    return pl.pallas_call(
        _final_norm_kernel,
        grid=(n // tm,),
        in_specs=[pl.BlockSpec((tm, d), lambda i: (i, 0)), pl.BlockSpec((1, d), lambda i: (0, 0))],
        out_specs=pl.BlockSpec((tm, d), lambda i: (i, 0)),
        out_shape=jax.ShapeDtypeStruct((n, d), F32),
        compiler_params=pltpu.CompilerParams(dimension_semantics=("arbitrary",)),
        name="final_norm",
    )(x2, g)


def _prefix_kernel(x_ref, sel_ref, o_ref):
    o_ref[...] = _dot3(x_ref[...], sel_ref[...])


def _prefix(x, sel, *, tm):
    n, k = x.shape
    w = sel.shape[1]
    assert n % tm == 0
    return pl.pallas_call(
        _prefix_kernel,
        grid=(n // tm,),
        in_specs=[pl.BlockSpec((tm, k), lambda i: (i, 0)), pl.BlockSpec((k, w), lambda i: (0, 0))],
        out_specs=pl.BlockSpec((tm, w), lambda i: (i, 0)),
        out_shape=jax.ShapeDtypeStruct((n, w), F32),
        compiler_params=pltpu.CompilerParams(dimension_semantics=("arbitrary",)),
        name="logf_prefix",
    )(x, sel)


def _block_diag_queries(q, n_rep, row_shift, lane_shift):
    qt = jnp.concatenate([q] * n_rep, axis=0)
    rg = lax.broadcasted_iota(jnp.int32, qt.shape, 0) >> row_shift
    lg = lax.broadcasted_iota(jnp.int32, qt.shape, 1) >> lane_shift
    return jnp.where(rg == lg, qt, 0.0).astype(BF16)


def _pad_rows(x, rows):
    return jnp.concatenate([x, jnp.zeros((rows - x.shape[0], x.shape[1]), x.dtype)], axis=0)


def _expand_heads(pieces, n_dec):
    sub = lax.broadcasted_iota(jnp.int32, (2 * n_dec, LANES), 0)
    out = []
    for p in range(len(pieces) // 2):
        a = jnp.broadcast_to(pieces[2 * p], (2 * n_dec, LANES))
        b = jnp.broadcast_to(pieces[2 * p + 1], (2 * n_dec, LANES))
        out.append(jnp.where(sub < n_dec, a, b))
    return jnp.concatenate(out, axis=0)


def _sample_stick(q, k_refs, vt_refs, kn_ref, vn_ref, acc_ref, n_dec):
    n_pages = len(k_refs)
    n_rows = HA * n_dec
    qbd = _block_diag_queries(q, HA, 2, 6)
    rowi = lax.broadcasted_iota(jnp.int32, (n_rows, LANES), 0)
    lane = lax.broadcasted_iota(jnp.int32, (n_rows, LANES), 1)
    strict = lane < (rowi & (n_dec - 1))
    key, key2 = _tile_iotas(LANES)
    later = (key > key2).astype(BF16)
    zs = [_dot(qbd, k_refs[r][0, 0].astype(BF16)) for r in range(n_pages)]
    zs.append(_dot_nt(qbd, _pad_rows(kn_ref[0], LANES).astype(BF16)))
    ls = [_log2_sigmoid(z) for z in zs]
    lk = [l - z for l, z in zip(ls, zs)]
    lk[-1] = jnp.where(strict, lk[-1], 0.0)
    his = [t.astype(BF16) for t in lk]
    los = [(t - h.astype(F32)).astype(BF16) for t, h in zip(lk, his)]
    cum = _dot(jnp.concatenate(his + los, axis=0), later)
    n_t = n_pages + 1
    cums = [cum[i * n_rows:(i + 1) * n_rows] + cum[(n_t + i) * n_rows:(n_t + i + 1) * n_rows] for i in range(n_t)]
    sums = [jnp.sum(t, axis=1, keepdims=True) for t in lk]
    w = jnp.where(strict, jnp.exp2(ls[-1] + cums[-1]), 0.0)
    acc = _dot(w.astype(BF16), _pad_rows(vn_ref[0], LANES).astype(BF16))
    c = sums[-1]
    for r in range(n_pages - 1, -1, -1):
        w = jnp.exp2(ls[r] + cums[r] + c)
        acc = acc + _dot_nt(w.astype(BF16), vt_refs[r][0, 0].astype(BF16))
        c = c + sums[r]
    acc_ref[...] = acc


def _sample_softmax(scores, vt_refs, vn_ref, acc_ref, l_ref):
    m = jnp.max(functools.reduce(jnp.maximum, scores), axis=1, keepdims=True)
    ps = [jnp.exp2(s - m) for s in scores]
    l_ref[...] = jnp.sum(functools.reduce(jnp.add, ps), axis=1, keepdims=True)
    acc = _dot(ps[-1].astype(BF16), _pad_rows(vn_ref[0], LANES).astype(BF16))
    for r, vt_ref in enumerate(vt_refs):
        acc = acc + _dot_nt(ps[r].astype(BF16), vt_ref[0, 0].astype(BF16))
    acc_ref[...] = acc


def _sample_kernel(pt_ref, q_ref, *refs, n_dec, n_pages, lam_init):
    del pt_ref
    refs = list(refs)
    take = lambda n: [refs.pop(0) for _ in range(n)]
    ka, va, kb, vb, kc, vc, fp = (take(n_pages) for _ in range(7))
    kna, vna, knb, vnb, knc, vnc, fn_ref, dl_ref, dn_ref = take(9)
    oa_ref, ob_ref, oc_ref = take(3)
    acc_a, acc_b, l_b, acc_c, l_c = take(5)
    q = q_ref[0]

    _sample_stick(q[:, 0:WA], ka, va, kna, vna, acc_a, n_dec)

    n_rows = HB * n_dec
    qbd = _block_diag_queries(q[:, WA:WA + WB], HB, 2, 6)
    rowi = lax.broadcasted_iota(jnp.int32, (n_rows, LANES), 0)
    lane = lax.broadcasted_iota(jnp.int32, (n_rows, LANES), 1)
    qpos = rowi & (n_dec - 1)
    fnew = _expand_heads([fn_ref[0][:, h * LANES:(h + 1) * LANES] for h in range(HB)], n_dec)
    cs = jnp.sum(jnp.where(lane == qpos, fnew, 0.0), axis=1, keepdims=True)
    s_new = _dot_nt(qbd, _pad_rows(knb[0], LANES).astype(BF16)) + (cs - fnew) * LOG2E
    scores = [None] * n_pages + [jnp.where(lane <= qpos, s_new, NEG_INF)]
    for r in range(n_pages - 1, -1, -1):
        f = fp[r][0, 0]
        fpg = _expand_heads([f[h:h + 1, :] for h in range(HB)], n_dec)
        cs = cs + fpg[:, LANES - 1:LANES]
        scores[r] = _dot(qbd, kb[r][0, 0].astype(BF16)) + (cs - fpg) * LOG2E
    _sample_softmax(scores, vb, vnb, acc_b, l_b)

    n_rows = HC * 2 * n_dec
    qbd = _block_diag_queries(q[:, WA + WB:], HC * 2, 2, 5)
    rowi = lax.broadcasted_iota(jnp.int32, (n_rows, LANES), 0)
    lane = lax.broadcasted_iota(jnp.int32, (n_rows, LANES), 1)
    scores = [_dot(qbd, kc[r][0, 0].astype(BF16)) for r in range(n_pages)]
    scores.append(jnp.where(lane <= (rowi & (n_dec - 1)), _dot_nt(qbd, _pad_rows(knc[0], LANES).astype(BF16)), NEG_INF))
    _sample_softmax(scores, vc, vnc, acc_c, l_c)

    laneh = lax.broadcasted_iota(jnp.int32, (n_dec, WA), 1) >> 6
    out = jnp.zeros((n_dec, WA), F32)
    for h in range(HA):
        out = jnp.where(laneh == h, acc_a[h * n_dec:(h + 1) * n_dec, :], out)
    oa_ref[0] = out

    laneh = lax.broadcasted_iota(jnp.int32, (n_dec, WB), 1) >> 6
    out = jnp.zeros((n_dec, WB), F32)
    for h in range(HB):
        rows = slice(h * n_dec, (h + 1) * n_dec)
        out = jnp.where(laneh == h, acc_b[rows, :] / l_b[rows, :], out)
    ob_ref[0] = out

    lam = _diff_lambda(dl_ref, lam_init)
    out = jnp.zeros((n_dec, WC), F32)
    for h in range(HC):
        r1 = slice((2 * h) * n_dec, (2 * h + 1) * n_dec)
        r2 = slice((2 * h + 1) * n_dec, (2 * h + 2) * n_dec)
        o = acc_c[r1, :] / l_c[r1, :] - lam * (acc_c[r2, :] / l_c[r2, :])
        mine = laneh == h
        ms = jnp.sum(jnp.where(mine, o * o, 0.0), axis=1, keepdims=True) * (1.0 / HEAD_DIM)
        out = jnp.where(mine, o * lax.rsqrt(ms + EPS), out)
    oc_ref[0] = out * dn_ref[...] * (1.0 - lam_init)


def _sample_attn(page_table, q3, caches_t, fpage, new_rows, fnew, dl, dn, layer, *, lam_init):
    nb, n_dec, d = q3.shape
    assert 2 * n_dec == SUBLANES, "two heads' query rows must fill one 8-sublane tile"
    n_pages = page_table.shape[1]
    page = caches_t[0].shape[3]
    assert page == LANES
    per_b = lambda b, pt: (b, 0, 0)
    const = lambda b, pt: (0, 0)
    in_specs = [pl.BlockSpec((1, n_dec, d), per_b)]
    args = [q3]
    for c in tuple(caches_t) + (fpage,):
        for r in range(n_pages):
            in_specs.append(pl.BlockSpec((1, 1, c.shape[2], page), lambda b, pt, r=r: (layer, pt[b, r], 0, 0)))
            args.append(c)
    for t in new_rows:
        in_specs.append(pl.BlockSpec((1, n_dec, t.shape[2]), per_b))
        args.append(t)
    in_specs += [pl.BlockSpec((1, 1, HB * LANES), per_b), pl.BlockSpec((4, QK_C), const),
                 pl.BlockSpec((1, WC), const)]
    args += [fnew, dl, dn]
    widths = (WA, WB, WC)
    scratch = [pltpu.VMEM((HA * n_dec, WA), F32),
               pltpu.VMEM((HB * n_dec, WB), F32), pltpu.VMEM((HB * n_dec, 1), F32),
               pltpu.VMEM((HC * 2 * n_dec, WC), F32), pltpu.VMEM((HC * 2 * n_dec, 1), F32)]
    grid_spec = pltpu.PrefetchScalarGridSpec(
        num_scalar_prefetch=1,
        grid=(nb,),
        in_specs=in_specs,
        out_specs=tuple(pl.BlockSpec((1, n_dec, w), per_b) for w in widths),
        scratch_shapes=scratch,
    )
    return pl.pallas_call(
        functools.partial(_sample_kernel, n_dec=n_dec, n_pages=n_pages, lam_init=lam_init),
        grid_spec=grid_spec,
        out_shape=tuple(jax.ShapeDtypeStruct((nb, n_dec, w), F32) for w in widths),
        compiler_params=pltpu.CompilerParams(dimension_semantics=("arbitrary",), vmem_limit_bytes=VMEM_LIMIT),
        name="sample_attn",
    )(page_table, *args)


def _rotate_half_columns(w):
    lead = w.shape[:-1]
    g = w.reshape(lead + (w.shape[-1] // QK_C, 2, QK_C // 2))
    return jnp.concatenate([-g[..., 1:2, :], g[..., 0:1, :]], axis=-2).reshape(w.shape)


def _arrange_w_in(w_in):
    sizes = [WA] * 3 + [WB] * 3 + [HB] + [WC] * 3
    bounds = [0]
    for s in sizes:
        bounds.append(bounds[-1] + s)
    qa, ka, va, qb, kb, vb, fb, qc, kc, vc = [w_in[..., bounds[i]:bounds[i + 1]] for i in range(len(sizes))]
    pad = jnp.zeros(w_in.shape[:-1] + (LANES - HB,), w_in.dtype)
    cols = [qa, ka, va, qb, kb, vb, qc, kc, vc, _rotate_half_columns(qc), _rotate_half_columns(kc), fb, pad]
    w_arr = jnp.concatenate(cols, axis=-1).astype(BF16)
    w_vt = jnp.swapaxes(jnp.concatenate([va, vb, vc], axis=-1), -1, -2).astype(BF16)
    return w_arr, w_vt


def _rope_tables(pos):
    half = QK_C // 2
    inv = ROPE_THETA ** (-jnp.arange(half, dtype=F32) / half)
    ang = pos.astype(F32)[:, None] * inv[None, :]
    reps = WC // half
    return jnp.tile(jnp.cos(ang), (1, reps)), jnp.tile(jnp.sin(ang), (1, reps))


def _new_prefix_matrix(n_dec, n_heads):
    r = jnp.arange(n_dec * LANES)
    c = jnp.arange(n_heads * LANES)
    m = (((r % LANES)[:, None] == (c // LANES)[None, :]) & ((r // LANES)[:, None] <= (c % LANES)[None, :])
         & ((c % LANES)[None, :] < n_dec))
    return m.astype(BF16)


def _keys_on_lanes(cache):
    l, pool, page, h, hd = cache.shape
    return jnp.transpose(cache, (0, 1, 3, 4, 2)).reshape(l, pool, h * hd, page)


def kernel(x_prompt, x_sample, cache_a_k, cache_a_v, cache_b_k, cache_b_v, cache_b_logf, cache_c_k, cache_c_v,
           page_table, w_in, b_forget, w_out, g_attn, diff_lambda, diff_norm, g_ffn, w_group, b_group,
           w_expert_router, b_expert_router, w_up_gate, w_up, w_down, g_final):
    nb, nt, d = x_prompt.shape
    sb, n_dec, _ = x_sample.shape
    depth = w_in.shape[0]
    n_pool, page = cache_a_k.shape[1], cache_a_k.shape[2]
    n_pages = page_table.shape[1]
    assert page == LANES
    tm = min(512, nt)
    tq = min(256, nt)

    w_arr, w_vt = _arrange_w_in(w_in)
    bf_pad = jnp.pad(b_forget.astype(F32), ((0, 0), (0, LANES - HB)))[:, None, :]
    wo = w_out.astype(BF16)
    wr_t = jnp.zeros((depth, _R_ROWS, d), F32)
    wr_t = wr_t.at[:, 0:N_GROUPS].set(jnp.swapaxes(w_group, 1, 2).astype(F32))
    wr_t = wr_t.at[:, _R_EXP0:_R_EXP0 + N_GROUPS * EXPERTS_PER_GROUP].set(
        jnp.swapaxes(w_expert_router, 1, 2).astype(F32))
    br = jnp.zeros((depth, _R_ROWS), F32)
    br = br.at[:, 0:N_GROUPS].set(b_group.astype(F32))
    br = br.at[:, _R_EXP0:_R_EXP0 + N_GROUPS * EXPERTS_PER_GROUP].set(b_expert_router.astype(F32))[:, :, None]
    w1 = w_up_gate.astype(BF16)
    w3 = w_up.astype(BF16)
    w2 = w_down.astype(BF16)
    dn_col = diff_norm.astype(F32)[:, :, None]
    dn_row = jnp.tile(diff_norm.astype(F32), (1, HC))[:, None, :]
    dl = diff_lambda.astype(F32)

    cos_p, sin_p = _rope_tables(jnp.arange(nt))
    pos_s = n_pages * page + jnp.arange(n_dec)
    cos_s, sin_s = _rope_tables(jnp.tile(pos_s, sb))

    caches_t = tuple(_keys_on_lanes(c.astype(F32)) for c in
                     (cache_a_k, cache_a_v, cache_b_k, cache_b_v, cache_c_k, cache_c_v))
    lf_rows = jnp.transpose(cache_b_logf.astype(F32), (0, 3, 1, 2)).reshape(depth * HB * n_pool, page)
    key, key2 = jnp.arange(page)[:, None], jnp.arange(page)[None, :]
    fpage = _prefix(lf_rows, (key <= key2).astype(BF16), tm=math.gcd(depth * HB * n_pool, 512))
    fpage = jnp.transpose(fpage.reshape(depth, HB, n_pool, page), (0, 2, 1, 3))
    sel_new = _new_prefix_matrix(n_dec, HB)

    xp = x_prompt.reshape(nb * nt, d).astype(F32)
    xs = x_sample.reshape(sb * n_dec, d).astype(F32)
    ns = sb * n_dec
    rows_p = [[] for _ in range(7)]
    rows_s = [[] for _ in range(7)]
    for l in range(depth):
        lam_init = 0.8 - 0.6 * math.exp(-0.3 * l)
        g_a = g_attn[l].astype(F32)[None, :]
        g_f = g_ffn[l].astype(F32)[None, :]

        q, ka, vat, kb, vbt, lf, lfp, kc, vct = _inproj(xp, g_a, w_arr[l], w_vt[l], bf_pad[l], cos_p, sin_p,
                                                        tm=tm, n_pos_tiles=nt // tm, n_batch=nb)
        b3 = lambda t: t.reshape(nb, nt, t.shape[-1])
        q3 = b3(q)
        oa = _prompt_attn("a", q3, b3(ka), vat, (), tq=tq)
        ob = _prompt_attn("b", q3, b3(kb), vbt, (b3(lfp),), tq=tq)
        oc = _prompt_attn("c", q3, b3(kc), vct, (dl[l], dn_col[l]), tq=tq, lam_init=lam_init)
        flat = lambda t: t.reshape(nb * nt, t.shape[-1])
        xp = _outproj_moe(xp, flat(oa), flat(ob), flat(oc), wo[l], g_f, wr_t[l], br[l], w1, w3, w2, l, tm=tm)
        for i, t in enumerate((ka, vat, kb, vbt, lf, kc, vct)):
            rows_p[i].append(t)

        q, ka, va, kb, vb, lf, lfp, kc, vc = _inproj(xs, g_a, w_arr[l], None, bf_pad[l], cos_s, sin_s,
                                                     tm=ns, n_pos_tiles=1)
        s3 = lambda t: t.reshape(sb, n_dec, t.shape[-1])
        fnew = _prefix(lfp.reshape(sb, n_dec * LANES), sel_new, tm=sb).reshape(sb, 1, HB * LANES)
        oa, ob, oc = _sample_attn(page_table, s3(q.astype(F32)), caches_t, fpage,
                                  tuple(s3(t) for t in (ka, va, kb, vb, kc, vc)), fnew, dl[l], dn_row[l], l,
                                  lam_init=lam_init)
        sflat = lambda t: t.reshape(ns, t.shape[-1]).astype(BF16)
        xs = _outproj_moe(xs, sflat(oa), sflat(ob), sflat(oc), wo[l], g_f, wr_t[l], br[l], w1, w3, w2, l, tm=ns)
        for i, t in enumerate((ka, va, kb, vb, lf, kc, vc)):
            rows_s[i].append(t)

    g_fin = g_final.astype(F32)[None, :]
    y_prompt = _final_norm(xp, g_fin, tm=tm).reshape(nb, nt, d)
    y_sample = _final_norm(xs, g_fin, tm=ns).reshape(sb, n_dec, d)

    def stack(rows, b, t, transposed):
        outs = []
        for i, r in enumerate(rows):
            a = jnp.stack(r, axis=0)
            if i == 4:
                outs.append(a.reshape(depth, b, t, HB))
            elif transposed and i in (1, 3, 6):
                a = a.reshape(depth, b, a.shape[2] // HEAD_DIM, HEAD_DIM, t)
                outs.append(jnp.transpose(a, (0, 1, 4, 2, 3)))
            else:
                outs.append(a.reshape(depth, b, t, a.shape[-1] // HEAD_DIM, HEAD_DIM))
        return outs

    return (y_prompt, y_sample, *stack(rows_p, nb, nt, True), *stack(rows_s, sb, n_dec, False))
```

```python
import functools
import math

import jax
import jax.numpy as jnp
from jax import lax
from jax.experimental import pallas as pl
from jax.experimental.pallas import tpu as pltpu

F32 = jnp.float32
BF16 = jnp.bfloat16

HEAD_DIM = 64
HA, HB, HC = 4, 6, 6
QK_C = HEAD_DIM // 2
WA, WB, WC = HA * HEAD_DIM, HB * HEAD_DIM, HC * HEAD_DIM
ROPE_THETA = 10000.0
N_GROUPS = 4
EXPERTS_PER_GROUP = 4
EPS = 1e-6
LOG2E = 1.4426950408889634
LANES = 128
SUBLANES = 8
HEADS_PER_BLOCK = LANES // HEAD_DIM
VMEM_LIMIT = 56 * 1024 * 1024
NEG_INF = float("-inf")

_C_QA, _C_KA, _C_VA = 0, WA, 2 * WA
_C_QB = 3 * WA
_C_KB, _C_VB = _C_QB + WB, _C_QB + 2 * WB
_C_QC = _C_QB + 3 * WB
_C_KC, _C_VC = _C_QC + WC, _C_QC + 2 * WC
_C_QCR, _C_KCR = _C_QC + 3 * WC, _C_QC + 4 * WC
_C_F = _C_QC + 5 * WC
_P_ARR = _C_F + LANES


def _dot(a, b):
    return jnp.dot(a, b, preferred_element_type=F32)


def _dot_nt(a, b):
    return lax.dot_general(a, b, (((1,), (1,)), ((), ())), preferred_element_type=F32)


def _log_sigmoid(z):
    return jnp.minimum(z, 0.0) - jnp.log(1.0 + jnp.exp(-jnp.abs(z)))


def _log2_sigmoid(z2):
    return jnp.minimum(z2, 0.0) - jnp.log2(1.0 + jnp.exp2(-jnp.abs(z2)))


def _split3(x):
    hi = x.astype(BF16)
    r = x - hi.astype(F32)
    mid = r.astype(BF16)
    lo = (r - mid.astype(F32)).astype(BF16)
    return hi, mid, lo


def _dot3(x, m):
    hi, mid, lo = _split3(x)
    return _dot(hi, m) + _dot(mid, m) + _dot(lo, m)


def _cumsum_rows(x, incl):
    hi, mid, lo = _split3(x)
    return _dot(incl, hi) + _dot(incl, mid) + _dot(incl, lo)


def _inproj_common(x_ref, g_ref, w_ref, bf_ref, cos_ref, sin_ref, q_ref, ka_ref, kb_ref, lf_ref, lfp_ref, kc_ref):
    x = x_ref[...]
    ms = jnp.mean(x * x, axis=-1, keepdims=True)
    h = (x * lax.rsqrt(ms + EPS) * g_ref[...]).astype(BF16)

    def proj(c0, width):
        return _dot(h, w_ref[:, c0:c0 + width])

    q_ref[:, 0:WA] = (proj(_C_QA, WA) * (HEAD_DIM ** -0.5 * LOG2E)).astype(BF16)
    ka_ref[...] = proj(_C_KA, WA)
    q_ref[:, WA:WA + WB] = (proj(_C_QB, WB) * (HEAD_DIM ** -0.5 * LOG2E)).astype(BF16)
    kb_ref[...] = proj(_C_KB, WB)
    cos = cos_ref[...]
    sin = sin_ref[...]
    qc = proj(_C_QC, WC) * cos + proj(_C_QCR, WC) * sin
    q_ref[:, WA + WB:] = (qc * (QK_C ** -0.5 * LOG2E)).astype(BF16)
    kc_ref[...] = proj(_C_KC, WC) * cos + proj(_C_KCR, WC) * sin
    lf = _log_sigmoid(proj(_C_F, LANES) + bf_ref[...])
    lane = lax.broadcasted_iota(jnp.int32, lf.shape, 1)
    lfp_ref[...] = jnp.where(lane < HB, lf, 0.0)
    lf_ref[...] = lf[:, :HB]
    return h, proj


def _inproj_rows_kernel(x_ref, g_ref, w_ref, bf_ref, cos_ref, sin_ref,
                        q_ref, ka_ref, va_ref, kb_ref, vb_ref, lf_ref, lfp_ref, kc_ref, vc_ref):
    _, proj = _inproj_common(x_ref, g_ref, w_ref, bf_ref, cos_ref, sin_ref,
                             q_ref, ka_ref, kb_ref, lf_ref, lfp_ref, kc_ref)
    va_ref[...] = proj(_C_VA, WA)
    vb_ref[...] = proj(_C_VB, WB)
    vc_ref[...] = proj(_C_VC, WC)


def _inproj_vt_kernel(x_ref, g_ref, w_ref, wvt_ref, bf_ref, cos_ref, sin_ref, *refs):
    q_ref, ka_ref, vat_ref, kb_ref, vbt_ref, lf_ref, lfp_ref, kc_ref, vct_ref = refs[-9:]
    h, _ = _inproj_common(x_ref, g_ref, w_ref, bf_ref, cos_ref, sin_ref,
                          q_ref, ka_ref, kb_ref, lf_ref, lfp_ref, kc_ref)
    vat_ref[0] = _dot_nt(wvt_ref[0:WA, :], h)
    vbt_ref[0] = _dot_nt(wvt_ref[WA:WA + WB, :], h)
    vct_ref[0] = _dot_nt(wvt_ref[WA + WB:, :], h)


_STACKED = (1, 2, 3, 4, 5, 7, 8)


def _inproj(x2, g, w_arr, w_vt, bf_pad, cos_t, sin_t, *, tm, n_pos_tiles, n_batch=None, layer=0, depth=1,
            stacked=None):
    n, d = x2.shape
    assert n % tm == 0
    grid = (n // tm,)
    row = lambda i: (i, 0)
    const = lambda i: (0, 0)
    pos = lambda i: (i % n_pos_tiles, 0)
    rows = lambda w, dt=F32: (jax.ShapeDtypeStruct((n, w), dt), pl.BlockSpec((tm, w), row))
    in_specs = [pl.BlockSpec((tm, d), row), pl.BlockSpec((1, d), const), pl.BlockSpec((d, _P_ARR), const)]
    args = [x2, g, w_arr]
    if w_vt is None:
        body = _inproj_rows_kernel
        outs = [rows(WA + WB + WC, BF16), rows(WA), rows(WA), rows(WB), rows(WB), rows(HB), rows(LANES), rows(WC),
                rows(WC)]
    else:
        body = _inproj_vt_kernel
        in_specs.append(pl.BlockSpec(w_vt.shape, const))
        args.append(w_vt)
        nt = n // n_batch
        srow = lambda w: (jax.ShapeDtypeStruct((depth, n, w), F32),
                          pl.BlockSpec((None, tm, w), lambda i: (layer, i, 0)))
        sval = lambda w: (jax.ShapeDtypeStruct((depth, n_batch, w, nt), F32),
                          pl.BlockSpec((None, 1, w, tm), lambda i: (layer, i // n_pos_tiles, 0, i % n_pos_tiles)))
        outs = [rows(WA + WB + WC, BF16), srow(WA), sval(WA), srow(WB), sval(WB), srow(HB), rows(LANES), srow(WC),
                sval(WC)]
    in_specs += [pl.BlockSpec((1, LANES), const), pl.BlockSpec((tm, WC), pos), pl.BlockSpec((tm, WC), pos)]
    args += [bf_pad, cos_t, sin_t]
    aliases = {}
    if stacked is not None:
        for arr, out_idx in zip(stacked, _STACKED):
            aliases[len(args)] = out_idx
            in_specs.append(pl.BlockSpec(memory_space=pl.ANY))
            args.append(arr)
    return pl.pallas_call(
        body,
        grid=grid,
        in_specs=in_specs,
        out_specs=tuple(o[1] for o in outs),
        out_shape=tuple(o[0] for o in outs),
        input_output_aliases=aliases,
        compiler_params=pltpu.CompilerParams(dimension_semantics=("arbitrary",), vmem_limit_bytes=VMEM_LIMIT),
        name="inproj",
    )(*args)


def _tile_iotas(t):
    return lax.broadcasted_iota(jnp.int32, (t, t), 0), lax.broadcasted_iota(jnp.int32, (t, t), 1)


def _masked_queries(q, width):
    lane = lax.broadcasted_iota(jnp.int32, q.shape, 1)
    zero = jnp.zeros_like(q)
    return jnp.concatenate([jnp.where((lane >= g * width) & (lane < (g + 1) * width), q, zero)
                            for g in range(LANES // width)], axis=0)


def _cast_kv(qi, k_ref, vt_ref, kb_ref, vtb_ref):
    @pl.when(qi == 0)
    def _():
        kb_ref[...] = k_ref[0].astype(BF16)
        vtb_ref[...] = vt_ref[0].astype(BF16)


def _sweep(qi, tq, kb_ref, qs, s_ref, update, init):
    def scores(kj):
        kt = kb_ref[pl.ds(pl.multiple_of(kj * tq, tq), tq), :]
        half = qs.shape[0] // 2
        return jnp.concatenate([_dot_nt(kt, qs[:half]), _dot_nt(kt, qs[half:])], axis=1)

    odd = qi % 2
    first = qi - 1 - odd

    def prefetch_first():
        s_ref[...] = scores(jnp.maximum(first, 0))

    state = lax.cond(
        odd == 1,
        lambda: update([qi, qi - 1], [scores(qi), scores(jnp.maximum(qi - 1, 0))], init, True, prefetch_first),
        lambda: update([qi], [scores(qi)], init, True, prefetch_first))

    def body(it, st):
        kj = first - 2 * it
        s_a = s_ref[...]
        s_b = scores(kj - 1)

        def prefetch():
            s_ref[...] = scores(jnp.maximum(kj - 2, 0))

        return update([kj, kj - 1], [s_a, s_b], st, False, prefetch)

    return lax.fori_loop(0, qi // 2, body, state)


def _attn_a_kernel(q_ref, k_ref, vt_ref, o_ref, kb_ref, vtb_ref, s_ref, *, tq):
    qi = pl.program_id(2)
    _cast_kv(qi, k_ref, vt_ref, kb_ref, vtb_ref)
    qs = _masked_queries(q_ref[0], HEAD_DIM)
    key, qry = _tile_iotas(tq)
    strict = key < qry
    later = (qry > key).astype(BF16)

    def tiles(kjs, z_alls, carry, diag, prefetch):
        n_g = len(kjs) * HEADS_PER_BLOCK
        zs = [z_all[:, hh * tq:(hh + 1) * tq] for z_all in z_alls for hh in range(HEADS_PER_BLOCK)]
        ls = [_log2_sigmoid(z) for z in zs]
        lk = [l - z for l, z in zip(ls, zs)]
        n_masked = HEADS_PER_BLOCK if diag else 0
        lk = [jnp.where(strict, t, 0.0) if g < n_masked else t for g, t in enumerate(lk)]
        prefetch()
        cum = _dot(later, jnp.concatenate([t.astype(BF16) for t in lk], axis=1))
        ws = [jnp.exp2(ls[g] + cum[:, g * tq:(g + 1) * tq]) for g in range(n_g)]
        ws = [jnp.where(strict, w, 0.0) if g < n_masked else w for g, w in enumerate(ws)]
        new = list(carry)
        for t, kj in enumerate(kjs):
            vt = vtb_ref[:, pl.ds(pl.multiple_of(kj * tq, tq), tq)]
            for hh in range(HEADS_PER_BLOCK):
                g = t * HEADS_PER_BLOCK + hh
                c, acc = new[hh]
                acc = acc + _dot(vt[hh * HEAD_DIM:(hh + 1) * HEAD_DIM, :], ws[g].astype(BF16)) * jnp.exp2(c)
                new[hh] = (c + jnp.sum(lk[g], axis=0, keepdims=True), acc)
        return tuple(new)

    init = tuple((jnp.zeros((1, tq), F32), jnp.zeros((HEAD_DIM, tq), F32)) for _ in range(HEADS_PER_BLOCK))
    carry = _sweep(qi, tq, kb_ref, qs, s_ref, tiles, init)
    o_t = jnp.concatenate([carry[hh][1] for hh in range(HEADS_PER_BLOCK)], axis=0)
    o_ref[0] = o_t.T.astype(BF16)


def _softmax_step(s, state, vt_rows, masked, causal):
    m, l, acc = state
    if masked:
        s = jnp.where(causal, s, NEG_INF)
    m_new = jnp.maximum(m, jnp.max(s, axis=0, keepdims=True))
    alpha = jnp.exp2(m - m_new)
    p = jnp.exp2(s - m_new)
    l = alpha * l + jnp.sum(p, axis=0, keepdims=True)
    acc = alpha * acc + _dot(vt_rows, p.astype(BF16))
    return m_new, l, acc


def _softmax_init(tq):
    return (jnp.full((1, tq), NEG_INF, F32), jnp.zeros((1, tq), F32), jnp.zeros((HEAD_DIM, tq), F32))


_BIAS_LANES = 8


def _attn_b_kernel(q_ref, k_ref, vt_ref, lf_ref, o_ref, kb_ref, vtb_ref, s_ref, qb_ref, *, tq, n_t):
    pair = pl.program_id(1)
    qi = pl.program_id(2)
    key, qry = _tile_iotas(tq)

    @pl.when(qi == 0)
    def _():
        kb_ref[:, 0:LANES] = k_ref[0].astype(BF16)
        vtb_ref[...] = vt_ref[0].astype(BF16)
        incl = (qry <= key).astype(BF16)
        src, dst = _tile_iotas(LANES)
        lane = lax.broadcasted_iota(jnp.int32, (tq, LANES), 1)
        sel_k, sel_q = [], []
        ones_k = jnp.zeros((tq, LANES), F32)
        neg_q = jnp.zeros((tq, LANES), F32)
        for part in range(3):
            mk = mq = None
            for hh in range(HEADS_PER_BLOCK):
                hg = pair * HEADS_PER_BLOCK + hh
                k_lane = hh * _BIAS_LANES + part
                q_lane = hh * _BIAS_LANES + 3 + part
                a = (src == hg) & (dst == k_lane)
                b = (src == hg) & (dst == q_lane)
                mk = a if mk is None else mk | a
                mq = b if mq is None else mq | b
                ones_k = jnp.where(lane == q_lane, 1.0, ones_k)
                neg_q = jnp.where(lane == k_lane, -1.0, neg_q)
            sel_k.append(mk.astype(BF16))
            sel_q.append(mq.astype(BF16))
        carry = jnp.zeros((1, LANES), F32)
        for c in range(n_t // tq):
            f = _cumsum_rows(lf_ref[0, c * tq:(c + 1) * tq, :], incl) + carry
            carry = f[tq - 1:tq, :]
            parts = _split3(f * LOG2E)
            k_bias = ones_k + functools.reduce(jnp.add, [_dot(p, m) for p, m in zip(parts, sel_k)])
            q_bias = neg_q + functools.reduce(jnp.add, [_dot(p, m) for p, m in zip(parts, sel_q)])
            kb_ref[c * tq:(c + 1) * tq, LANES:] = k_bias.astype(BF16)
            qb_ref[c * tq:(c + 1) * tq, :] = q_bias.astype(BF16)

    q = q_ref[0]
    qb = qb_ref[pl.ds(pl.multiple_of(qi * tq, tq), tq), :]
    lane = lax.broadcasted_iota(jnp.int32, q.shape, 1)
    zero = jnp.zeros_like(q)
    qs = jnp.concatenate(
        [jnp.concatenate([jnp.where((lane >= hh * HEAD_DIM) & (lane < (hh + 1) * HEAD_DIM), q, zero),
                          jnp.where((lane >= hh * _BIAS_LANES) & (lane < (hh + 1) * _BIAS_LANES), qb, zero)],
                         axis=1) for hh in range(HEADS_PER_BLOCK)], axis=0)
    causal = key <= qry

    def tiles(kjs, s_alls, carry, diag, prefetch):
        for t, (kj, s_all) in enumerate(zip(kjs, s_alls)):
            if t == len(kjs) - 1:
                prefetch()
            vt = vtb_ref[:, pl.ds(pl.multiple_of(kj * tq, tq), tq)]
            new = []
            for hh in range(HEADS_PER_BLOCK):
                s = s_all[:, hh * tq:(hh + 1) * tq]
                new.append(_softmax_step(s, carry[hh], vt[hh * HEAD_DIM:(hh + 1) * HEAD_DIM, :],
                                         diag and t == 0, causal))
            carry = tuple(new)
        return carry

    carry = _sweep(qi, tq, kb_ref, qs, s_ref, tiles, tuple(_softmax_init(tq) for _ in range(HEADS_PER_BLOCK)))
    o_t = jnp.concatenate([carry[hh][2] / carry[hh][1] for hh in range(HEADS_PER_BLOCK)], axis=0)
    o_ref[0] = o_t.T.astype(BF16)


def _diff_lambda(dl_ref, lam_init):
    dl = dl_ref[...]
    a = jnp.sum(dl[0:1, :] * dl[1:2, :], axis=1, keepdims=True)
    b = jnp.sum(dl[2:3, :] * dl[3:4, :], axis=1, keepdims=True)
    return jnp.exp(a) - jnp.exp(b) + lam_init


def _attn_c_kernel(q_ref, k_ref, vt_ref, dl_ref, dn_ref, o_ref, kb_ref, vtb_ref, s_ref, *, tq, lam_init):
    qi = pl.program_id(2)
    _cast_kv(qi, k_ref, vt_ref, kb_ref, vtb_ref)
    qs = _masked_queries(q_ref[0], QK_C)
    n_maps = LANES // QK_C
    key, qry = _tile_iotas(tq)
    causal = key <= qry

    def tiles(kjs, s_alls, carry, diag, prefetch):
        for t, (kj, s_all) in enumerate(zip(kjs, s_alls)):
            if t == len(kjs) - 1:
                prefetch()
            vt = vtb_ref[:, pl.ds(pl.multiple_of(kj * tq, tq), tq)]
            new = []
            for mp in range(n_maps):
                hh = mp // 2
                s = s_all[:, mp * tq:(mp + 1) * tq]
                new.append(_softmax_step(s, carry[mp], vt[hh * HEAD_DIM:(hh + 1) * HEAD_DIM, :],
                                         diag and t == 0, causal))
            carry = tuple(new)
        return carry

    carry = _sweep(qi, tq, kb_ref, qs, s_ref, tiles, tuple(_softmax_init(tq) for _ in range(n_maps)))
    lam = _diff_lambda(dl_ref, lam_init)
    outs = []
    for hh in range(HEADS_PER_BLOCK):
        m1, m2 = carry[2 * hh], carry[2 * hh + 1]
        o = m1[2] / m1[1] - lam * (m2[2] / m2[1])
        ms = jnp.mean(o * o, axis=0, keepdims=True)
        outs.append(o * lax.rsqrt(ms + EPS) * dn_ref[...] * (1.0 - lam_init))
    o_ref[0] = jnp.concatenate(outs, axis=0).T.astype(BF16)


def _prompt_attn(kind, q_all, k, vt, extra, layer, *, tq, lam_init=None):
    _, nb, nt, w = k.shape
    assert nt % tq == 0 and tq % LANES == 0
    n_pairs = w // LANES
    q_blk0 = {"a": 0, "b": WA // LANES, "c": (WA + WB) // LANES}[kind]
    grid = (nb, n_pairs, nt // tq)
    q_spec = pl.BlockSpec((1, tq, LANES), lambda b, p, i: (b, i, q_blk0 + p))
    k_spec = pl.BlockSpec((None, 1, nt, LANES), lambda b, p, i: (layer, b, 0, p))
    vt_spec = pl.BlockSpec((None, 1, LANES, nt), lambda b, p, i: (layer, b, p, 0))
    o_spec = pl.BlockSpec((1, tq, LANES), lambda b, p, i: (b, i, p))
    n_groups = LANES // (QK_C if kind == "c" else HEAD_DIM)
    k_width = 2 * LANES if kind == "b" else LANES
    scratch = [pltpu.VMEM((nt, k_width), BF16), pltpu.VMEM((LANES, nt), BF16), pltpu.VMEM((tq, n_groups * tq), F32)]
    in_specs = [q_spec, k_spec, vt_spec]
    if kind == "a":
        body = functools.partial(_attn_a_kernel, tq=tq)
    elif kind == "b":
        body = functools.partial(_attn_b_kernel, tq=tq, n_t=nt)
        in_specs.append(pl.BlockSpec((1, nt, LANES), lambda b, p, i: (b, 0, 0)))
        scratch.append(pltpu.VMEM((nt, LANES), BF16))
    else:
        body = functools.partial(_attn_c_kernel, tq=tq, lam_init=lam_init)
        in_specs += [pl.BlockSpec((4, QK_C), lambda b, p, i: (0, 0)),
                     pl.BlockSpec((HEAD_DIM, 1), lambda b, p, i: (0, 0))]
    return pl.pallas_call(
        body,
        grid=grid,
        in_specs=in_specs,
        out_specs=o_spec,
        out_shape=jax.ShapeDtypeStruct((nb, nt, w), BF16),
        scratch_shapes=scratch,
        compiler_params=pltpu.CompilerParams(dimension_semantics=("arbitrary", "arbitrary", "arbitrary"),
                                             vmem_limit_bytes=VMEM_LIMIT),
        name="prompt_attn_" + kind,
    )(q_all, k, vt, *extra)


_R_EXP0 = 8
_R_ROWS = 32


def _router_rows(lt):
    g = [lt[j:j + 1, :] for j in range(N_GROUPS)]
    gmax = functools.reduce(jnp.maximum, g)
    idx = jnp.full(gmax.shape, N_GROUPS - 1, jnp.int32)
    for j in range(N_GROUPS - 2, -1, -1):
        idx = jnp.where(g[j] == gmax, j, idx)
    g_top = 1.0 / functools.reduce(jnp.add, [jnp.exp(gj - gmax) for gj in g])
    e_in = []
    for j in range(EXPERTS_PER_GROUP):
        r = _R_EXP0 + (N_GROUPS - 1) * EXPERTS_PER_GROUP + j
        ej = lt[r:r + 1, :]
        for gg in range(N_GROUPS - 2, -1, -1):
            r = _R_EXP0 + gg * EXPERTS_PER_GROUP + j
            ej = jnp.where(idx == gg, lt[r:r + 1, :], ej)
        e_in.append(ej)
    emax = functools.reduce(jnp.maximum, e_in)
    u = [jnp.exp(ej - emax) for ej in e_in]
    usum = functools.reduce(jnp.add, u)
    p = [uj / usum for uj in u]
    top1 = functools.reduce(jnp.maximum, p)
    i1 = jnp.full(top1.shape, EXPERTS_PER_GROUP - 1, jnp.int32)
    for j in range(EXPERTS_PER_GROUP - 2, -1, -1):
        i1 = jnp.where(p[j] == top1, j, i1)
    rest = [jnp.where(i1 == j, -1.0, p[j]) for j in range(EXPERTS_PER_GROUP)]
    top2 = functools.reduce(jnp.maximum, rest)
    i2 = jnp.full(top2.shape, EXPERTS_PER_GROUP - 1, jnp.int32)
    for j in range(EXPERTS_PER_GROUP - 2, -1, -1):
        i2 = jnp.where(rest[j] == top2, j, i2)
    denom = top1 + top2
    gw = [jnp.where((i1 == j) | (i2 == j), p[j] / denom * g_top, 0.0) for j in range(EXPERTS_PER_GROUP)]
    return gw, idx


def _outproj_moe_kernel(x_ref, oa_ref, ob_ref, oc_ref, wo_ref, gf_ref, wr_ref, br_ref, w1_ref, w3_ref, w2_ref,
                        out_ref, xmid_ref, h2_ref, gate_ref, y_ref):
    g = pl.program_id(1)
    tm = x_ref.shape[0]

    @pl.when(g == 0)
    def _():
        xm = (x_ref[...] + _dot(oa_ref[...], wo_ref[0:WA, :]) + _dot(ob_ref[...], wo_ref[WA:WA + WB, :])
              + _dot(oc_ref[...], wo_ref[WA + WB:, :]))
        xmid_ref[...] = xm
        ms = jnp.mean(xm * xm, axis=-1, keepdims=True)
        h2 = xm * lax.rsqrt(ms + EPS) * gf_ref[...]
        h2_hi = h2.astype(BF16)
        h2_ref[...] = h2_hi
        h2_lo = (h2 - h2_hi.astype(F32)).astype(BF16)
        part = _dot_nt(wr_ref[...], h2_hi)
        lt = (part[0:_R_ROWS] + part[_R_ROWS:] + _dot_nt(wr_ref[0:_R_ROWS, :], h2_lo)) + br_ref[...]
        gw, idx = _router_rows(lt)
        sub = lax.broadcasted_iota(jnp.int32, (SUBLANES, tm), 0)
        rows = jnp.where(sub == EXPERTS_PER_GROUP, idx.astype(F32), 0.0)
        for j in range(EXPERTS_PER_GROUP):
            rows = jnp.where(sub == j, gw[j], rows)
        full = jnp.concatenate([rows, jnp.zeros((LANES - SUBLANES, tm), F32)], axis=0)
        gate_ref[...] = full.T
        y_ref[...] = jnp.zeros_like(y_ref)

    h2b = h2_ref[...]
    gate = gate_ref[...]
    mine = gate[:, EXPERTS_PER_GROUP:EXPERTS_PER_GROUP + 1] == g.astype(F32)
    y = y_ref[...]
    for e in range(EXPERTS_PER_GROUP):
        a = _dot(h2b, w1_ref[0, e])
        b = _dot(h2b, w3_ref[0, e])
        gcol = jnp.where(mine, gate[:, e:e + 1], 0.0)
        hid = a * (1.0 / (1.0 + jnp.exp(-a))) * b * gcol
        y = y + _dot(hid.astype(BF16), w2_ref[0, e])
    y_ref[...] = y

    @pl.when(g == N_GROUPS - 1)
    def _():
        out_ref[...] = xmid_ref[...] + y_ref[...]


def _outproj_moe(x2, oa, ob, oc, wo, gf, wr_t, br, w1, w3, w2, layer, *, tm):
    n, d = x2.shape
    assert n % tm == 0
    f = w1.shape[-1]
    grid = (n // tm, N_GROUPS)
    row = lambda i, g: (i, 0)
    const = lambda i, g: (0, 0)
    wsel = lambda i, g: (layer, g, 0, 0)
    return pl.pallas_call(
        _outproj_moe_kernel,
        grid=grid,
        in_specs=[
            pl.BlockSpec((tm, d), row),
            pl.BlockSpec((tm, WA), row), pl.BlockSpec((tm, WB), row), pl.BlockSpec((tm, WC), row),
            pl.BlockSpec((d, d), const),
            pl.BlockSpec((1, d), const),
            pl.BlockSpec((2 * _R_ROWS, d), const),
            pl.BlockSpec((_R_ROWS, 1), const),
            pl.BlockSpec((1, EXPERTS_PER_GROUP, d, f), wsel),
            pl.BlockSpec((1, EXPERTS_PER_GROUP, d, f), wsel),
            pl.BlockSpec((1, EXPERTS_PER_GROUP, f, d), wsel),
        ],
        out_specs=pl.BlockSpec((tm, d), row),
        out_shape=jax.ShapeDtypeStruct((n, d), F32),
        scratch_shapes=[pltpu.VMEM((tm, d), F32), pltpu.VMEM((tm, d), BF16), pltpu.VMEM((tm, LANES), F32),
                        pltpu.VMEM((tm, d), F32)],
        compiler_params=pltpu.CompilerParams(dimension_semantics=("arbitrary", "arbitrary"),
                                             vmem_limit_bytes=VMEM_LIMIT),
        name="outproj_moe",
    )(x2, oa, ob, oc, wo, gf, wr_t, br, w1, w3, w2)


def _final_norm_kernel(x_ref, g_ref, o_ref):
    x = x_ref[...]
    ms = jnp.mean(x * x, axis=-1, keepdims=True)
    o_ref[...] = x * lax.rsqrt(ms + EPS) * g_ref[...]


def _final_norm(x2, g, *, tm):
    n, d = x2.shape
    assert n % tm == 0
    return pl.pallas_call(
        _final_norm_kernel,
        grid=(n // tm,),
        in_specs=[pl.BlockSpec((tm, d), lambda i: (i, 0)), pl.BlockSpec((1, d), lambda i: (0, 0))],
        out_specs=pl.BlockSpec((tm, d), lambda i: (i, 0)),
        out_shape=jax.ShapeDtypeStruct((n, d), F32),
        compiler_params=pltpu.CompilerParams(dimension_semantics=("arbitrary",)),
        name="final_norm",
    )(x2, g)


def _prefix_kernel(x_ref, sel_ref, o_ref):
    o_ref[...] = _dot3(x_ref[...], sel_ref[...])


def _prefix(x, sel, *, tm):
    n, k = x.shape
    w = sel.shape[1]
    assert n % tm == 0
    return pl.pallas_call(
        _prefix_kernel,
        grid=(n // tm,),
        in_specs=[pl.BlockSpec((tm, k), lambda i: (i, 0)), pl.BlockSpec((k, w), lambda i: (0, 0))],
        out_specs=pl.BlockSpec((tm, w), lambda i: (i, 0)),
        out_shape=jax.ShapeDtypeStruct((n, w), F32),
        compiler_params=pltpu.CompilerParams(dimension_semantics=("arbitrary",)),
        name="logf_prefix",
    )(x, sel)


def _block_diag_queries(q, n_rep, row_shift, lane_shift):
    qt = jnp.concatenate([q] * n_rep, axis=0)
    rg = lax.broadcasted_iota(jnp.int32, qt.shape, 0) >> row_shift
    lg = lax.broadcasted_iota(jnp.int32, qt.shape, 1) >> lane_shift
    return jnp.where(rg == lg, qt, 0.0).astype(BF16)


def _pad_rows(x, rows):
    return jnp.concatenate([x, jnp.zeros((rows - x.shape[0], x.shape[1]), x.dtype)], axis=0)


def _expand_heads(pieces, n_dec):
    sub = lax.broadcasted_iota(jnp.int32, (2 * n_dec, LANES), 0)
    out = []
    for p in range(len(pieces) // 2):
        a = jnp.broadcast_to(pieces[2 * p], (2 * n_dec, LANES))
        b = jnp.broadcast_to(pieces[2 * p + 1], (2 * n_dec, LANES))
        out.append(jnp.where(sub < n_dec, a, b))
    return jnp.concatenate(out, axis=0)


def _sample_stick(q, k_refs, vt_refs, kn_ref, vn_ref, acc_ref, n_dec):
    n_pages = len(k_refs)
    n_rows = HA * n_dec
    qbd = _block_diag_queries(q, HA, 2, 6)
    rowi = lax.broadcasted_iota(jnp.int32, (n_rows, LANES), 0)
    lane = lax.broadcasted_iota(jnp.int32, (n_rows, LANES), 1)
    strict = lane < (rowi & (n_dec - 1))
    key, key2 = _tile_iotas(LANES)
    later = (key > key2).astype(BF16)
    zs = [_dot(qbd, k_refs[r][0, 0].astype(BF16)) for r in range(n_pages)]
    zs.append(_dot_nt(qbd, _pad_rows(kn_ref[0], LANES).astype(BF16)))
    ls = [_log2_sigmoid(z) for z in zs]
    lk = [l - z for l, z in zip(ls, zs)]
    lk[-1] = jnp.where(strict, lk[-1], 0.0)
    his = [t.astype(BF16) for t in lk]
    los = [(t - h.astype(F32)).astype(BF16) for t, h in zip(lk, his)]
    cum = _dot(jnp.concatenate(his + los, axis=0), later)
    n_t = n_pages + 1
    cums = [cum[i * n_rows:(i + 1) * n_rows] + cum[(n_t + i) * n_rows:(n_t + i + 1) * n_rows] for i in range(n_t)]
    sums = [jnp.sum(t, axis=1, keepdims=True) for t in lk]
    w = jnp.where(strict, jnp.exp2(ls[-1] + cums[-1]), 0.0)
    acc = _dot(w.astype(BF16), _pad_rows(vn_ref[0], LANES).astype(BF16))
    c = sums[-1]
    for r in range(n_pages - 1, -1, -1):
        w = jnp.exp2(ls[r] + cums[r] + c)
        acc = acc + _dot_nt(w.astype(BF16), vt_refs[r][0, 0].astype(BF16))
        c = c + sums[r]
    acc_ref[...] = acc


def _sample_softmax(scores, vt_refs, vn_ref, acc_ref, l_ref):
    m = jnp.max(functools.reduce(jnp.maximum, scores), axis=1, keepdims=True)
    ps = [jnp.exp2(s - m) for s in scores]
    l_ref[...] = jnp.sum(functools.reduce(jnp.add, ps), axis=1, keepdims=True)
    acc = _dot(ps[-1].astype(BF16), _pad_rows(vn_ref[0], LANES).astype(BF16))
    for r, vt_ref in enumerate(vt_refs):
        acc = acc + _dot_nt(ps[r].astype(BF16), vt_ref[0, 0].astype(BF16))
    acc_ref[...] = acc


def _sample_kernel(pt_ref, q_ref, *refs, n_dec, n_pages, lam_init):
    del pt_ref
    refs = list(refs)
    take = lambda n: [refs.pop(0) for _ in range(n)]
    ka, va, kb, vb, kc, vc, fp = (take(n_pages) for _ in range(7))
    kna, vna, knb, vnb, knc, vnc, fn_ref, dl_ref, dn_ref = take(9)
    oa_ref, ob_ref, oc_ref = take(3)
    acc_a, acc_b, l_b, acc_c, l_c = take(5)
    q = q_ref[0]

    _sample_stick(q[:, 0:WA], ka, va, kna, vna, acc_a, n_dec)

    n_rows = HB * n_dec
    qbd = _block_diag_queries(q[:, WA:WA + WB], HB, 2, 6)
    rowi = lax.broadcasted_iota(jnp.int32, (n_rows, LANES), 0)
    lane = lax.broadcasted_iota(jnp.int32, (n_rows, LANES), 1)
    qpos = rowi & (n_dec - 1)
    fnew = _expand_heads([fn_ref[0][:, h * LANES:(h + 1) * LANES] for h in range(HB)], n_dec)
    cs = jnp.sum(jnp.where(lane == qpos, fnew, 0.0), axis=1, keepdims=True)
    s_new = _dot_nt(qbd, _pad_rows(knb[0], LANES).astype(BF16)) + (cs - fnew) * LOG2E
    scores = [None] * n_pages + [jnp.where(lane <= qpos, s_new, NEG_INF)]
    for r in range(n_pages - 1, -1, -1):
        f = fp[r][0, 0]
        fpg = _expand_heads([f[h:h + 1, :] for h in range(HB)], n_dec)
        cs = cs + fpg[:, LANES - 1:LANES]
        scores[r] = _dot(qbd, kb[r][0, 0].astype(BF16)) + (cs - fpg) * LOG2E
    _sample_softmax(scores, vb, vnb, acc_b, l_b)

    n_rows = HC * 2 * n_dec
    qbd = _block_diag_queries(q[:, WA + WB:], HC * 2, 2, 5)
    rowi = lax.broadcasted_iota(jnp.int32, (n_rows, LANES), 0)
    lane = lax.broadcasted_iota(jnp.int32, (n_rows, LANES), 1)
    scores = [_dot(qbd, kc[r][0, 0].astype(BF16)) for r in range(n_pages)]
    scores.append(jnp.where(lane <= (rowi & (n_dec - 1)), _dot_nt(qbd, _pad_rows(knc[0], LANES).astype(BF16)), NEG_INF))
    _sample_softmax(scores, vc, vnc, acc_c, l_c)

    laneh = lax.broadcasted_iota(jnp.int32, (n_dec, WA), 1) >> 6
    out = jnp.zeros((n_dec, WA), F32)
    for h in range(HA):
        out = jnp.where(laneh == h, acc_a[h * n_dec:(h + 1) * n_dec, :], out)
    oa_ref[0] = out

    laneh = lax.broadcasted_iota(jnp.int32, (n_dec, WB), 1) >> 6
    out = jnp.zeros((n_dec, WB), F32)
    for h in range(HB):
        rows = slice(h * n_dec, (h + 1) * n_dec)
        out = jnp.where(laneh == h, acc_b[rows, :] / l_b[rows, :], out)
    ob_ref[0] = out

    lam = _diff_lambda(dl_ref, lam_init)
    out = jnp.zeros((n_dec, WC), F32)
    for h in range(HC):
        r1 = slice((2 * h) * n_dec, (2 * h + 1) * n_dec)
        r2 = slice((2 * h + 1) * n_dec, (2 * h + 2) * n_dec)
        o = acc_c[r1, :] / l_c[r1, :] - lam * (acc_c[r2, :] / l_c[r2, :])
        mine = laneh == h
        ms = jnp.sum(jnp.where(mine, o * o, 0.0), axis=1, keepdims=True) * (1.0 / HEAD_DIM)
        out = jnp.where(mine, o * lax.rsqrt(ms + EPS), out)
    oc_ref[0] = out * dn_ref[...] * (1.0 - lam_init)


def _sample_attn(page_table, q3, caches_t, fpage, new_rows, fnew, dl, dn, layer, *, lam_init):
    nb, n_dec, d = q3.shape
    assert 2 * n_dec == SUBLANES, "two heads' query rows must fill one 8-sublane tile"
    n_pages = page_table.shape[1]
    page = caches_t[0].shape[3]
    assert page == LANES
    per_b = lambda b, pt: (b, 0, 0)
    const = lambda b, pt: (0, 0)
    in_specs = [pl.BlockSpec((1, n_dec, d), per_b)]
    args = [q3]
    for c in tuple(caches_t) + (fpage,):
        for r in range(n_pages):
            in_specs.append(pl.BlockSpec((1, 1, c.shape[2], page), lambda b, pt, r=r: (layer, pt[b, r], 0, 0)))
            args.append(c)
    for t in new_rows:
        in_specs.append(pl.BlockSpec((1, n_dec, t.shape[2]), per_b))
        args.append(t)
    in_specs += [pl.BlockSpec((1, 1, HB * LANES), per_b), pl.BlockSpec((4, QK_C), const),
                 pl.BlockSpec((1, WC), const)]
    args += [fnew, dl, dn]
    widths = (WA, WB, WC)
    scratch = [pltpu.VMEM((HA * n_dec, WA), F32),
               pltpu.VMEM((HB * n_dec, WB), F32), pltpu.VMEM((HB * n_dec, 1), F32),
               pltpu.VMEM((HC * 2 * n_dec, WC), F32), pltpu.VMEM((HC * 2 * n_dec, 1), F32)]
    grid_spec = pltpu.PrefetchScalarGridSpec(
        num_scalar_prefetch=1,
        grid=(nb,),
        in_specs=in_specs,
        out_specs=tuple(pl.BlockSpec((1, n_dec, w), per_b) for w in widths),
        scratch_shapes=scratch,
    )
    return pl.pallas_call(
        functools.partial(_sample_kernel, n_dec=n_dec, n_pages=n_pages, lam_init=lam_init),
        grid_spec=grid_spec,
        out_shape=tuple(jax.ShapeDtypeStruct((nb, n_dec, w), F32) for w in widths),
        compiler_params=pltpu.CompilerParams(dimension_semantics=("arbitrary",), vmem_limit_bytes=VMEM_LIMIT),
        name="sample_attn",
    )(page_table, *args)


def _rotate_half_columns(w):
    lead = w.shape[:-1]
    g = w.reshape(lead + (w.shape[-1] // QK_C, 2, QK_C // 2))
    return jnp.concatenate([-g[..., 1:2, :], g[..., 0:1, :]], axis=-2).reshape(w.shape)


def _arrange_w_in(w_in):
    sizes = [WA] * 3 + [WB] * 3 + [HB] + [WC] * 3
    bounds = [0]
    for s in sizes:
        bounds.append(bounds[-1] + s)
    qa, ka, va, qb, kb, vb, fb, qc, kc, vc = [w_in[..., bounds[i]:bounds[i + 1]] for i in range(len(sizes))]
    pad = jnp.zeros(w_in.shape[:-1] + (LANES - HB,), w_in.dtype)
    cols = [qa, ka, va, qb, kb, vb, qc, kc, vc, _rotate_half_columns(qc), _rotate_half_columns(kc), fb, pad]
    w_arr = jnp.concatenate(cols, axis=-1).astype(BF16)
    w_vt = jnp.swapaxes(jnp.concatenate([va, vb, vc], axis=-1), -1, -2).astype(BF16)
    return w_arr, w_vt


def _rope_tables(pos):
    half = QK_C // 2
    inv = ROPE_THETA ** (-jnp.arange(half, dtype=F32) / half)
    ang = pos.astype(F32)[:, None] * inv[None, :]
    reps = WC // half
    return jnp.tile(jnp.cos(ang), (1, reps)), jnp.tile(jnp.sin(ang), (1, reps))


def _new_prefix_matrix(n_dec, n_heads):
    r = jnp.arange(n_dec * LANES)
    c = jnp.arange(n_heads * LANES)
    m = (((r % LANES)[:, None] == (c // LANES)[None, :]) & ((r // LANES)[:, None] <= (c % LANES)[None, :])
         & ((c % LANES)[None, :] < n_dec))
    return m.astype(BF16)


def _keys_on_lanes(cache):
    l, pool, page, h, hd = cache.shape
    return jnp.transpose(cache, (0, 1, 3, 4, 2)).reshape(l, pool, h * hd, page)


def kernel(x_prompt, x_sample, cache_a_k, cache_a_v, cache_b_k, cache_b_v, cache_b_logf, cache_c_k, cache_c_v,
           page_table, w_in, b_forget, w_out, g_attn, diff_lambda, diff_norm, g_ffn, w_group, b_group,
           w_expert_router, b_expert_router, w_up_gate, w_up, w_down, g_final):
    nb, nt, d = x_prompt.shape
    sb, n_dec, _ = x_sample.shape
    depth = w_in.shape[0]
    n_pool, page = cache_a_k.shape[1], cache_a_k.shape[2]
    n_pages = page_table.shape[1]
    assert page == LANES
    tm = min(512, nt)
    tq = min(256, nt)

    w_arr, w_vt = _arrange_w_in(w_in)
    bf_pad = jnp.pad(b_forget.astype(F32), ((0, 0), (0, LANES - HB)))[:, None, :]
    wo = w_out.astype(BF16)
    wr_t = jnp.zeros((depth, _R_ROWS, d), F32)
    wr_t = wr_t.at[:, 0:N_GROUPS].set(jnp.swapaxes(w_group, 1, 2).astype(F32))
    wr_t = wr_t.at[:, _R_EXP0:_R_EXP0 + N_GROUPS * EXPERTS_PER_GROUP].set(
        jnp.swapaxes(w_expert_router, 1, 2).astype(F32))
    wr_hi = wr_t.astype(BF16)
    wr_t = jnp.concatenate([wr_hi, (wr_t - wr_hi.astype(F32)).astype(BF16)], axis=1)
    br = jnp.zeros((depth, _R_ROWS), F32)
    br = br.at[:, 0:N_GROUPS].set(b_group.astype(F32))
    br = br.at[:, _R_EXP0:_R_EXP0 + N_GROUPS * EXPERTS_PER_GROUP].set(b_expert_router.astype(F32))[:, :, None]
    w1 = w_up_gate.astype(BF16)
    w3 = w_up.astype(BF16)
    w2 = w_down.astype(BF16)
    dn_col = diff_norm.astype(F32)[:, :, None]
    dn_row = jnp.tile(diff_norm.astype(F32), (1, HC))[:, None, :]
    dl = diff_lambda.astype(F32)

    cos_p, sin_p = _rope_tables(jnp.arange(nt))
    pos_s = n_pages * page + jnp.arange(n_dec)
    cos_s, sin_s = _rope_tables(jnp.tile(pos_s, sb))

    caches_t = tuple(_keys_on_lanes(c.astype(F32)) for c in
                     (cache_a_k, cache_a_v, cache_b_k, cache_b_v, cache_c_k, cache_c_v))
    lf_rows = jnp.transpose(cache_b_logf.astype(F32), (0, 3, 1, 2)).reshape(depth * HB * n_pool, page)
    key, key2 = jnp.arange(page)[:, None], jnp.arange(page)[None, :]
    fpage = _prefix(lf_rows, (key <= key2).astype(BF16), tm=math.gcd(depth * HB * n_pool, 4096))
    fpage = jnp.transpose(fpage.reshape(depth, HB, n_pool, page), (0, 2, 1, 3))
    sel_new = _new_prefix_matrix(n_dec, HB)

    xp = x_prompt.reshape(nb * nt, d).astype(F32)
    xs = x_sample.reshape(sb * n_dec, d).astype(F32)
    ns = sb * n_dec
    rows_p = None
    rows_s = [[] for _ in range(7)]
    for l in range(depth):
        lam_init = 0.8 - 0.6 * math.exp(-0.3 * l)
        g_a = g_attn[l].astype(F32)[None, :]
        g_f = g_ffn[l].astype(F32)[None, :]

        outs = _inproj(xp, g_a, w_arr[l], w_vt[l], bf_pad[l], cos_p, sin_p, tm=tm, n_pos_tiles=nt // tm,
                       n_batch=nb, layer=l, depth=depth, stacked=rows_p)
        q, lfp = outs[0], outs[6]
        rows_p = tuple(outs[i] for i in _STACKED)
        ka, vat, kb, vbt, _, kc, vct = rows_p
        b3 = lambda t: t.reshape(nb, nt, t.shape[-1])
        b4 = lambda t: t.reshape(depth, nb, nt, t.shape[-1])
        q3 = b3(q)
        oa = _prompt_attn("a", q3, b4(ka), vat, (), l, tq=tq)
        ob = _prompt_attn("b", q3, b4(kb), vbt, (b3(lfp),), l, tq=tq)
        oc = _prompt_attn("c", q3, b4(kc), vct, (dl[l], dn_col[l]), l, tq=tq, lam_init=lam_init)
        flat = lambda t: t.reshape(nb * nt, t.shape[-1])
        xp = _outproj_moe(xp, flat(oa), flat(ob), flat(oc), wo[l], g_f, wr_t[l], br[l], w1, w3, w2, l, tm=tm)

        q, ka, va, kb, vb, lf, lfp, kc, vc = _inproj(xs, g_a, w_arr[l], None, bf_pad[l], cos_s, sin_s,
                                                     tm=ns, n_pos_tiles=1)
        s3 = lambda t: t.reshape(sb, n_dec, t.shape[-1])
        fnew = _prefix(lfp.reshape(sb, n_dec * LANES), sel_new, tm=sb).reshape(sb, 1, HB * LANES)
        oa, ob, oc = _sample_attn(page_table, s3(q.astype(F32)), caches_t, fpage,
                                  tuple(s3(t) for t in (ka, va, kb, vb, kc, vc)), fnew, dl[l], dn_row[l], l,
                                  lam_init=lam_init)
        sflat = lambda t: t.reshape(ns, t.shape[-1]).astype(BF16)
        xs = _outproj_moe(xs, sflat(oa), sflat(ob), sflat(oc), wo[l], g_f, wr_t[l], br[l], w1, w3, w2, l, tm=ns)
        for i, t in enumerate((ka, va, kb, vb, lf, kc, vc)):
            rows_s[i].append(t)

    g_fin = g_final.astype(F32)[None, :]
    y_prompt = _final_norm(xp, g_fin, tm=tm).reshape(nb, nt, d)
    y_sample = _final_norm(xs, g_fin, tm=ns).reshape(sb, n_dec, d)

    def heads(arrays, b, t, transposed):
        outs = []
        for i, a in enumerate(arrays):
            if i == 4:
                outs.append(a.reshape(depth, b, t, HB))
            elif transposed and i in (1, 3, 6):
                a = a.reshape(depth, b, a.shape[2] // HEAD_DIM, HEAD_DIM, t)
                outs.append(jnp.transpose(a, (0, 1, 4, 2, 3)))
            else:
                outs.append(a.reshape(depth, b, t, a.shape[-1] // HEAD_DIM, HEAD_DIM))
        return outs

    rows_s = [jnp.stack(r, axis=0) for r in rows_s]
    return (y_prompt, y_sample, *heads(rows_p, nb, nt, True), *heads(rows_s, sb, n_dec, False))
```

```python
import functools
import math

import jax
import jax.numpy as jnp
from jax import lax
from jax.experimental import pallas as pl
from jax.experimental.pallas import tpu as pltpu

F32 = jnp.float32
BF16 = jnp.bfloat16

HEAD_DIM = 64
HA, HB, HC = 4, 6, 6
QK_C = HEAD_DIM // 2
WA, WB, WC = HA * HEAD_DIM, HB * HEAD_DIM, HC * HEAD_DIM
ROPE_THETA = 10000.0
N_GROUPS = 4
EXPERTS_PER_GROUP = 4
EPS = 1e-6
LOG2E = 1.4426950408889634
LANES = 128
SUBLANES = 8
HEADS_PER_BLOCK = LANES // HEAD_DIM
VMEM_LIMIT = 56 * 1024 * 1024
NEG_INF = float("-inf")

_C_QA, _C_KA, _C_VA = 0, WA, 2 * WA
_C_QB = 3 * WA
_C_KB, _C_VB = _C_QB + WB, _C_QB + 2 * WB
_C_QC = _C_QB + 3 * WB
_C_KC, _C_VC = _C_QC + WC, _C_QC + 2 * WC
_C_QCR, _C_KCR = _C_QC + 3 * WC, _C_QC + 4 * WC
_C_F = _C_QC + 5 * WC
_P_ARR = _C_F + LANES


def _dot(a, b):
    return jnp.dot(a, b, preferred_element_type=F32)


def _dot_nt(a, b):
    return lax.dot_general(a, b, (((1,), (1,)), ((), ())), preferred_element_type=F32)


def _log_sigmoid(z):
    return jnp.minimum(z, 0.0) - jnp.log(1.0 + jnp.exp(-jnp.abs(z)))


def _log2_sigmoid(z2):
    return jnp.minimum(z2, 0.0) - jnp.log2(1.0 + jnp.exp2(-jnp.abs(z2)))


def _split3(x):
    hi = x.astype(BF16)
    r = x - hi.astype(F32)
    mid = r.astype(BF16)
    lo = (r - mid.astype(F32)).astype(BF16)
    return hi, mid, lo


def _dot3(x, m):
    hi, mid, lo = _split3(x)
    return _dot(hi, m) + _dot(mid, m) + _dot(lo, m)


def _cumsum_rows(x, incl):
    hi, mid, lo = _split3(x)
    return _dot(incl, hi) + _dot(incl, mid) + _dot(incl, lo)


def _inproj_common(x_ref, g_ref, w_ref, bf_ref, cos_ref, sin_ref, q_ref, ka_ref, kb_ref, lf_ref, lfp_ref, kc_ref):
    x = x_ref[...]
    ms = jnp.mean(x * x, axis=-1, keepdims=True)
    h = (x * lax.rsqrt(ms + EPS) * g_ref[...]).astype(BF16)

    def proj(c0, width):
        return _dot(h, w_ref[:, c0:c0 + width])

    q_ref[:, 0:WA] = (proj(_C_QA, WA) * (HEAD_DIM ** -0.5 * LOG2E)).astype(BF16)
    ka_ref[...] = proj(_C_KA, WA)
    q_ref[:, WA:WA + WB] = (proj(_C_QB, WB) * (HEAD_DIM ** -0.5 * LOG2E)).astype(BF16)
    kb_ref[...] = proj(_C_KB, WB)
    cos = cos_ref[...]
    sin = sin_ref[...]
    qc = proj(_C_QC, WC) * cos + proj(_C_QCR, WC) * sin
    q_ref[:, WA + WB:] = (qc * (QK_C ** -0.5 * LOG2E)).astype(BF16)
    kc_ref[...] = proj(_C_KC, WC) * cos + proj(_C_KCR, WC) * sin
    lf = _log_sigmoid(proj(_C_F, LANES) + bf_ref[...])
    lane = lax.broadcasted_iota(jnp.int32, lf.shape, 1)
    lfp_ref[...] = jnp.where(lane < HB, lf, 0.0)
    lf_ref[...] = lf[:, :HB]
    return h, proj


def _inproj_rows_kernel(x_ref, g_ref, w_ref, bf_ref, cos_ref, sin_ref,
                        q_ref, ka_ref, va_ref, kb_ref, vb_ref, lf_ref, lfp_ref, kc_ref, vc_ref):
    _, proj = _inproj_common(x_ref, g_ref, w_ref, bf_ref, cos_ref, sin_ref,
                             q_ref, ka_ref, kb_ref, lf_ref, lfp_ref, kc_ref)
    va_ref[...] = proj(_C_VA, WA)
    vb_ref[...] = proj(_C_VB, WB)
    vc_ref[...] = proj(_C_VC, WC)


def _inproj_vt_kernel(x_ref, g_ref, w_ref, wvt_ref, bf_ref, cos_ref, sin_ref, *refs):
    q_ref, ka_ref, vat_ref, kb_ref, vbt_ref, lf_ref, lfp_ref, kc_ref, vct_ref = refs[-9:]
    h, _ = _inproj_common(x_ref, g_ref, w_ref, bf_ref, cos_ref, sin_ref,
                          q_ref, ka_ref, kb_ref, lf_ref, lfp_ref, kc_ref)
    vat_ref[0] = _dot_nt(wvt_ref[0:WA, :], h)
    vbt_ref[0] = _dot_nt(wvt_ref[WA:WA + WB, :], h)
    vct_ref[0] = _dot_nt(wvt_ref[WA + WB:, :], h)


_STACKED = (1, 2, 3, 4, 5, 7, 8)


def _inproj(x2, g, w_arr, w_vt, bf_pad, cos_t, sin_t, *, tm, n_pos_tiles, n_batch=None, layer=0, depth=1,
            stacked=None):
    n, d = x2.shape
    assert n % tm == 0
    grid = (n // tm,)
    row = lambda i: (i, 0)
    const = lambda i: (0, 0)
    pos = lambda i: (i % n_pos_tiles, 0)
    rows = lambda w, dt=F32: (jax.ShapeDtypeStruct((n, w), dt), pl.BlockSpec((tm, w), row))
    in_specs = [pl.BlockSpec((tm, d), row), pl.BlockSpec((1, d), const), pl.BlockSpec((d, _P_ARR), const)]
    args = [x2, g, w_arr]
    if w_vt is None:
        body = _inproj_rows_kernel
        outs = [rows(WA + WB + WC, BF16), rows(WA), rows(WA), rows(WB), rows(WB), rows(HB), rows(LANES), rows(WC),
                rows(WC)]
    else:
        body = _inproj_vt_kernel
        in_specs.append(pl.BlockSpec(w_vt.shape, const))
        args.append(w_vt)
        nt = n // n_batch
        srow = lambda w: (jax.ShapeDtypeStruct((depth, n, w), F32),
                          pl.BlockSpec((None, tm, w), lambda i: (layer, i, 0)))
        sval = lambda w: (jax.ShapeDtypeStruct((depth, n_batch, w, nt), F32),
                          pl.BlockSpec((None, 1, w, tm), lambda i: (layer, i // n_pos_tiles, 0, i % n_pos_tiles)))
        outs = [rows(WA + WB + WC, BF16), srow(WA), sval(WA), srow(WB), sval(WB), srow(HB), rows(LANES), srow(WC),
                sval(WC)]
    in_specs += [pl.BlockSpec((1, LANES), const), pl.BlockSpec((tm, WC), pos), pl.BlockSpec((tm, WC), pos)]
    args += [bf_pad, cos_t, sin_t]
    aliases = {}
    if stacked is not None:
        for arr, out_idx in zip(stacked, _STACKED):
            aliases[len(args)] = out_idx
            in_specs.append(pl.BlockSpec(memory_space=pl.ANY))
            args.append(arr)
    return pl.pallas_call(
        body,
        grid=grid,
        in_specs=in_specs,
        out_specs=tuple(o[1] for o in outs),
        out_shape=tuple(o[0] for o in outs),
        input_output_aliases=aliases,
        compiler_params=pltpu.CompilerParams(dimension_semantics=("arbitrary",), vmem_limit_bytes=VMEM_LIMIT),
        name="inproj",
    )(*args)


def _tile_iotas(t):
    return lax.broadcasted_iota(jnp.int32, (t, t), 0), lax.broadcasted_iota(jnp.int32, (t, t), 1)


def _masked_queries(q, width):
    lane = lax.broadcasted_iota(jnp.int32, q.shape, 1)
    zero = jnp.zeros_like(q)
    return jnp.concatenate([jnp.where((lane >= g * width) & (lane < (g + 1) * width), q, zero)
                            for g in range(LANES // width)], axis=0)


def _cast_kv(qi, k_ref, vt_ref, kb_ref, vtb_ref):
    @pl.when(qi == 0)
    def _():
        kb_ref[...] = k_ref[0].astype(BF16)
        vtb_ref[...] = vt_ref[0].astype(BF16)


def _sweep(qi, tq, kb_ref, qs, s_ref, update, init):
    def scores(kj):
        kt = kb_ref[pl.ds(pl.multiple_of(kj * tq, tq), tq), :]
        half = qs.shape[0] // 2
        return jnp.concatenate([_dot_nt(kt, qs[:half]), _dot_nt(kt, qs[half:])], axis=1)

    odd = qi % 2
    first = qi - 1 - odd

    def prefetch_first():
        s_ref[...] = scores(jnp.maximum(first, 0))

    state = lax.cond(
        odd == 1,
        lambda: update([qi, qi - 1], [scores(qi), scores(jnp.maximum(qi - 1, 0))], init, True, prefetch_first),
        lambda: update([qi], [scores(qi)], init, True, prefetch_first))

    def body(it, st):
        kj = first - 2 * it
        s_a = s_ref[...]
        s_b = scores(kj - 1)

        def prefetch():
            s_ref[...] = scores(jnp.maximum(kj - 2, 0))

        return update([kj, kj - 1], [s_a, s_b], st, False, prefetch)

    return lax.fori_loop(0, qi // 2, body, state)


def _attn_a_kernel(q_ref, k_ref, vt_ref, o_ref, kb_ref, vtb_ref, s_ref, *, tq):
    qi = pl.program_id(2)
    _cast_kv(qi, k_ref, vt_ref, kb_ref, vtb_ref)
    qs = _masked_queries(q_ref[0], HEAD_DIM)
    key, qry = _tile_iotas(tq)
    strict = key < qry
    later = (qry > key).astype(BF16)

    def tiles(kjs, z_alls, carry, diag, prefetch):
        n_g = len(kjs) * HEADS_PER_BLOCK
        zs = [z_all[:, hh * tq:(hh + 1) * tq] for z_all in z_alls for hh in range(HEADS_PER_BLOCK)]
        ls = [_log2_sigmoid(z) for z in zs]
        lk = [l - z for l, z in zip(ls, zs)]
        n_masked = HEADS_PER_BLOCK if diag else 0
        lk = [jnp.where(strict, t, 0.0) if g < n_masked else t for g, t in enumerate(lk)]
        prefetch()
        cum = _dot(later, jnp.concatenate([t.astype(BF16) for t in lk], axis=1))
        ws = [jnp.exp2(ls[g] + cum[:, g * tq:(g + 1) * tq]) for g in range(n_g)]
        ws = [jnp.where(strict, w, 0.0) if g < n_masked else w for g, w in enumerate(ws)]
        new = list(carry)
        for t, kj in enumerate(kjs):
            vt = vtb_ref[:, pl.ds(pl.multiple_of(kj * tq, tq), tq)]
            for hh in range(HEADS_PER_BLOCK):
                g = t * HEADS_PER_BLOCK + hh
                c, acc = new[hh]
                acc = acc + _dot(vt[hh * HEAD_DIM:(hh + 1) * HEAD_DIM, :], ws[g].astype(BF16)) * jnp.exp2(c)
                new[hh] = (c + jnp.sum(lk[g], axis=0, keepdims=True), acc)
        return tuple(new)

    init = tuple((jnp.zeros((1, tq), F32), jnp.zeros((HEAD_DIM, tq), F32)) for _ in range(HEADS_PER_BLOCK))
    carry = _sweep(qi, tq, kb_ref, qs, s_ref, tiles, init)
    o_t = jnp.concatenate([carry[hh][1] for hh in range(HEADS_PER_BLOCK)], axis=0)
    o_ref[0] = o_t.T.astype(BF16)


def _softmax_step(s, state, vt_rows, masked, causal):
    m, l, acc = state
    if masked:
        s = jnp.where(causal, s, NEG_INF)
    m_new = jnp.maximum(m, jnp.max(s, axis=0, keepdims=True))
    alpha = jnp.exp2(m - m_new)
    p = jnp.exp2(s - m_new)
    l = alpha * l + jnp.sum(p, axis=0, keepdims=True)
    acc = alpha * acc + _dot(vt_rows, p.astype(BF16))
    return m_new, l, acc


def _softmax_init(tq):
    return (jnp.full((1, tq), NEG_INF, F32), jnp.zeros((1, tq), F32), jnp.zeros((HEAD_DIM, tq), F32))


_BIAS_LANES = 8


def _attn_b_kernel(q_ref, k_ref, vt_ref, lf_ref, o_ref, kb_ref, vtb_ref, s_ref, qb_ref, *, tq, n_t):
    pair = pl.program_id(1)
    qi = pl.program_id(2)
    key, qry = _tile_iotas(tq)

    @pl.when(qi == 0)
    def _():
        kb_ref[:, 0:LANES] = k_ref[0].astype(BF16)
        vtb_ref[...] = vt_ref[0].astype(BF16)
        incl = (qry <= key).astype(BF16)
        src, dst = _tile_iotas(LANES)
        lane = lax.broadcasted_iota(jnp.int32, (tq, LANES), 1)
        sel_k, sel_q = [], []
        ones_k = jnp.zeros((tq, LANES), F32)
        neg_q = jnp.zeros((tq, LANES), F32)
        for part in range(3):
            mk = mq = None
            for hh in range(HEADS_PER_BLOCK):
                hg = pair * HEADS_PER_BLOCK + hh
                k_lane = hh * _BIAS_LANES + part
                q_lane = hh * _BIAS_LANES + 3 + part
                a = (src == hg) & (dst == k_lane)
                b = (src == hg) & (dst == q_lane)
                mk = a if mk is None else mk | a
                mq = b if mq is None else mq | b
                ones_k = jnp.where(lane == q_lane, 1.0, ones_k)
                neg_q = jnp.where(lane == k_lane, -1.0, neg_q)
            sel_k.append(mk.astype(BF16))
            sel_q.append(mq.astype(BF16))
        carry = jnp.zeros((1, LANES), F32)
        for c in range(n_t // tq):
            f = _cumsum_rows(lf_ref[0, c * tq:(c + 1) * tq, :], incl) + carry
            carry = f[tq - 1:tq, :]
            parts = _split3(f * LOG2E)
            k_bias = ones_k + functools.reduce(jnp.add, [_dot(p, m) for p, m in zip(parts, sel_k)])
            q_bias = neg_q + functools.reduce(jnp.add, [_dot(p, m) for p, m in zip(parts, sel_q)])
            kb_ref[c * tq:(c + 1) * tq, LANES:] = k_bias.astype(BF16)
            qb_ref[c * tq:(c + 1) * tq, :] = q_bias.astype(BF16)

    q = q_ref[0]
    qb = qb_ref[pl.ds(pl.multiple_of(qi * tq, tq), tq), :]
    lane = lax.broadcasted_iota(jnp.int32, q.shape, 1)
    zero = jnp.zeros_like(q)
    qs = jnp.concatenate(
        [jnp.concatenate([jnp.where((lane >= hh * HEAD_DIM) & (lane < (hh + 1) * HEAD_DIM), q, zero),
                          jnp.where((lane >= hh * _BIAS_LANES) & (lane < (hh + 1) * _BIAS_LANES), qb, zero)],
                         axis=1) for hh in range(HEADS_PER_BLOCK)], axis=0)
    causal = key <= qry

    def tiles(kjs, s_alls, carry, diag, prefetch):
        for t, (kj, s_all) in enumerate(zip(kjs, s_alls)):
            if t == len(kjs) - 1:
                prefetch()
            vt = vtb_ref[:, pl.ds(pl.multiple_of(kj * tq, tq), tq)]
            new = []
            for hh in range(HEADS_PER_BLOCK):
                s = s_all[:, hh * tq:(hh + 1) * tq]
                new.append(_softmax_step(s, carry[hh], vt[hh * HEAD_DIM:(hh + 1) * HEAD_DIM, :],
                                         diag and t == 0, causal))
            carry = tuple(new)
        return carry

    carry = _sweep(qi, tq, kb_ref, qs, s_ref, tiles, tuple(_softmax_init(tq) for _ in range(HEADS_PER_BLOCK)))
    o_t = jnp.concatenate([carry[hh][2] / carry[hh][1] for hh in range(HEADS_PER_BLOCK)], axis=0)
    o_ref[0] = o_t.T.astype(BF16)


def _diff_lambda(dl_ref, lam_init):
    dl = dl_ref[...]
    a = jnp.sum(dl[0:1, :] * dl[1:2, :], axis=1, keepdims=True)
    b = jnp.sum(dl[2:3, :] * dl[3:4, :], axis=1, keepdims=True)
    return jnp.exp(a) - jnp.exp(b) + lam_init


def _attn_c_kernel(q_ref, k_ref, vt_ref, dl_ref, dn_ref, o_ref, kb_ref, vtb_ref, s_ref, *, tq, lam_init):
    qi = pl.program_id(2)
    _cast_kv(qi, k_ref, vt_ref, kb_ref, vtb_ref)
    qs = _masked_queries(q_ref[0], QK_C)
    n_maps = LANES // QK_C
    key, qry = _tile_iotas(tq)
    causal = key <= qry

    def tiles(kjs, s_alls, carry, diag, prefetch):
        for t, (kj, s_all) in enumerate(zip(kjs, s_alls)):
            if t == len(kjs) - 1:
                prefetch()
            vt = vtb_ref[:, pl.ds(pl.multiple_of(kj * tq, tq), tq)]
            new = []
            for mp in range(n_maps):
                hh = mp // 2
                s = s_all[:, mp * tq:(mp + 1) * tq]
                new.append(_softmax_step(s, carry[mp], vt[hh * HEAD_DIM:(hh + 1) * HEAD_DIM, :],
                                         diag and t == 0, causal))
            carry = tuple(new)
        return carry

    carry = _sweep(qi, tq, kb_ref, qs, s_ref, tiles, tuple(_softmax_init(tq) for _ in range(n_maps)))
    lam = _diff_lambda(dl_ref, lam_init)
    outs = []
    for hh in range(HEADS_PER_BLOCK):
        m1, m2 = carry[2 * hh], carry[2 * hh + 1]
        o = m1[2] / m1[1] - lam * (m2[2] / m2[1])
        ms = jnp.mean(o * o, axis=0, keepdims=True)
        outs.append(o * lax.rsqrt(ms + EPS) * dn_ref[...] * (1.0 - lam_init))
    o_ref[0] = jnp.concatenate(outs, axis=0).T.astype(BF16)


def _prompt_attn(kind, q_all, k, vt, extra, layer, *, tq, lam_init=None):
    _, nb, nt, w = k.shape
    assert nt % tq == 0 and tq % LANES == 0
    n_pairs = w // LANES
    q_blk0 = {"a": 0, "b": WA // LANES, "c": (WA + WB) // LANES}[kind]
    grid = (nb, n_pairs, nt // tq)
    q_spec = pl.BlockSpec((1, tq, LANES), lambda b, p, i: (b, i, q_blk0 + p))
    k_spec = pl.BlockSpec((None, 1, nt, LANES), lambda b, p, i: (layer, b, 0, p))
    vt_spec = pl.BlockSpec((None, 1, LANES, nt), lambda b, p, i: (layer, b, p, 0))
    o_spec = pl.BlockSpec((1, tq, LANES), lambda b, p, i: (b, i, p))
    n_groups = LANES // (QK_C if kind == "c" else HEAD_DIM)
    k_width = 2 * LANES if kind == "b" else LANES
    scratch = [pltpu.VMEM((nt, k_width), BF16), pltpu.VMEM((LANES, nt), BF16), pltpu.VMEM((tq, n_groups * tq), F32)]
    in_specs = [q_spec, k_spec, vt_spec]
    if kind == "a":
        body = functools.partial(_attn_a_kernel, tq=tq)
    elif kind == "b":
        body = functools.partial(_attn_b_kernel, tq=tq, n_t=nt)
        in_specs.append(pl.BlockSpec((1, nt, LANES), lambda b, p, i: (b, 0, 0)))
        scratch.append(pltpu.VMEM((nt, LANES), BF16))
    else:
        body = functools.partial(_attn_c_kernel, tq=tq, lam_init=lam_init)
        in_specs += [pl.BlockSpec((4, QK_C), lambda b, p, i: (0, 0)),
                     pl.BlockSpec((HEAD_DIM, 1), lambda b, p, i: (0, 0))]
    return pl.pallas_call(
        body,
        grid=grid,
        in_specs=in_specs,
        out_specs=o_spec,
        out_shape=jax.ShapeDtypeStruct((nb, nt, w), BF16),
        scratch_shapes=scratch,
        compiler_params=pltpu.CompilerParams(dimension_semantics=("arbitrary", "arbitrary", "arbitrary"),
                                             vmem_limit_bytes=VMEM_LIMIT),
        name="prompt_attn_" + kind,
    )(q_all, k, vt, *extra)


_R_EXP0 = 8
_R_ROWS = 32


def _router_rows(lt):
    g = [lt[j:j + 1, :] for j in range(N_GROUPS)]
    gmax = functools.reduce(jnp.maximum, g)
    idx = jnp.full(gmax.shape, N_GROUPS - 1, jnp.int32)
    for j in range(N_GROUPS - 2, -1, -1):
        idx = jnp.where(g[j] == gmax, j, idx)
    g_top = 1.0 / functools.reduce(jnp.add, [jnp.exp(gj - gmax) for gj in g])
    e_in = []
    for j in range(EXPERTS_PER_GROUP):
        r = _R_EXP0 + (N_GROUPS - 1) * EXPERTS_PER_GROUP + j
        ej = lt[r:r + 1, :]
        for gg in range(N_GROUPS - 2, -1, -1):
            r = _R_EXP0 + gg * EXPERTS_PER_GROUP + j
            ej = jnp.where(idx == gg, lt[r:r + 1, :], ej)
        e_in.append(ej)
    emax = functools.reduce(jnp.maximum, e_in)
    u = [jnp.exp(ej - emax) for ej in e_in]
    usum = functools.reduce(jnp.add, u)
    p = [uj / usum for uj in u]
    top1 = functools.reduce(jnp.maximum, p)
    i1 = jnp.full(top1.shape, EXPERTS_PER_GROUP - 1, jnp.int32)
    for j in range(EXPERTS_PER_GROUP - 2, -1, -1):
        i1 = jnp.where(p[j] == top1, j, i1)
    rest = [jnp.where(i1 == j, -1.0, p[j]) for j in range(EXPERTS_PER_GROUP)]
    top2 = functools.reduce(jnp.maximum, rest)
    i2 = jnp.full(top2.shape, EXPERTS_PER_GROUP - 1, jnp.int32)
    for j in range(EXPERTS_PER_GROUP - 2, -1, -1):
        i2 = jnp.where(rest[j] == top2, j, i2)
    denom = top1 + top2
    gw = [jnp.where((i1 == j) | (i2 == j), p[j] / denom * g_top, 0.0) for j in range(EXPERTS_PER_GROUP)]
    return gw, idx


def _outproj_moe_kernel(x_ref, oa_ref, ob_ref, oc_ref, wo_ref, gf_ref, wr_ref, br_ref, w1_ref, w3_ref, w2_ref,
                        out_ref, xmid_ref, h2_ref, gate_ref, y_ref):
    g = pl.program_id(1)
    tm = x_ref.shape[0]

    @pl.when(g == 0)
    def _():
        xm = (x_ref[...] + _dot(oa_ref[...], wo_ref[0:WA, :]) + _dot(ob_ref[...], wo_ref[WA:WA + WB, :])
              + _dot(oc_ref[...], wo_ref[WA + WB:, :]))
        xmid_ref[...] = xm
        ms = jnp.mean(xm * xm, axis=-1, keepdims=True)
        h2 = xm * lax.rsqrt(ms + EPS) * gf_ref[...]
        h2_hi = h2.astype(BF16)
        h2_ref[...] = h2_hi
        h2_lo = (h2 - h2_hi.astype(F32)).astype(BF16)
        part = _dot_nt(wr_ref[...], h2_hi)
        lt = (part[0:_R_ROWS] + part[_R_ROWS:] + _dot_nt(wr_ref[0:_R_ROWS, :], h2_lo)) + br_ref[...]
        gw, idx = _router_rows(lt)
        sub = lax.broadcasted_iota(jnp.int32, (SUBLANES, tm), 0)
        rows = jnp.where(sub == EXPERTS_PER_GROUP, idx.astype(F32), 0.0)
        for j in range(EXPERTS_PER_GROUP):
            rows = jnp.where(sub == j, gw[j], rows)
        full = jnp.concatenate([rows, jnp.zeros((LANES - SUBLANES, tm), F32)], axis=0)
        gate_ref[...] = full.T
        y_ref[...] = jnp.zeros_like(y_ref)

    h2b = h2_ref[...]
    gate = gate_ref[...]
    mine = gate[:, EXPERTS_PER_GROUP:EXPERTS_PER_GROUP + 1] == g.astype(F32)
    y = y_ref[...]
    for e in range(EXPERTS_PER_GROUP):
        a = _dot(h2b, w1_ref[0, e])
        b = _dot(h2b, w3_ref[0, e])
        gcol = jnp.where(mine, gate[:, e:e + 1], 0.0)
        hid = a * (1.0 / (1.0 + jnp.exp(-a))) * b * gcol
        y = y + _dot(hid.astype(BF16), w2_ref[0, e])
    y_ref[...] = y

    @pl.when(g == N_GROUPS - 1)
    def _():
        out_ref[...] = xmid_ref[...] + y_ref[...]


def _outproj_moe(x2, oa, ob, oc, wo, gf, wr_t, br, w1, w3, w2, layer, *, tm):
    n, d = x2.shape
    assert n % tm == 0
    f = w1.shape[-1]
    grid = (n // tm, N_GROUPS)
    row = lambda i, g: (i, 0)
    const = lambda i, g: (0, 0)
    wsel = lambda i, g: (layer, g, 0, 0)
    return pl.pallas_call(
        _outproj_moe_kernel,
        grid=grid,
        in_specs=[
            pl.BlockSpec((tm, d), row),
            pl.BlockSpec((tm, WA), row), pl.BlockSpec((tm, WB), row), pl.BlockSpec((tm, WC), row),
            pl.BlockSpec((d, d), const),
            pl.BlockSpec((1, d), const),
            pl.BlockSpec((2 * _R_ROWS, d), const),
            pl.BlockSpec((_R_ROWS, 1), const),
            pl.BlockSpec((1, EXPERTS_PER_GROUP, d, f), wsel),
            pl.BlockSpec((1, EXPERTS_PER_GROUP, d, f), wsel),
            pl.BlockSpec((1, EXPERTS_PER_GROUP, f, d), wsel),
        ],
        out_specs=pl.BlockSpec((tm, d), row),
        out_shape=jax.ShapeDtypeStruct((n, d), F32),
        scratch_shapes=[pltpu.VMEM((tm, d), F32), pltpu.VMEM((tm, d), BF16), pltpu.VMEM((tm, LANES), F32),
                        pltpu.VMEM((tm, d), F32)],
        compiler_params=pltpu.CompilerParams(dimension_semantics=("arbitrary", "arbitrary"),
                                             vmem_limit_bytes=VMEM_LIMIT),
        name="outproj_moe",
    )(x2, oa, ob, oc, wo, gf, wr_t, br, w1, w3, w2)


def _final_norm_kernel(x_ref, g_ref, o_ref):
    x = x_ref[...]
    ms = jnp.mean(x * x, axis=-1, keepdims=True)
    o_ref[...] = x * lax.rsqrt(ms + EPS) * g_ref[...]


def _final_norm(x2, g, *, tm):
    n, d = x2.shape
    assert n % tm == 0
    return pl.pallas_call(
        _final_norm_kernel,
        grid=(n // tm,),
        in_specs=[pl.BlockSpec((tm, d), lambda i: (i, 0)), pl.BlockSpec((1, d), lambda i: (0, 0))],
        out_specs=pl.BlockSpec((tm, d), lambda i: (i, 0)),
        out_shape=jax.ShapeDtypeStruct((n, d), F32),
        compiler_params=pltpu.CompilerParams(dimension_semantics=("arbitrary",)),
        name="final_norm",
    )(x2, g)


def _prefix_kernel(x_ref, sel_ref, o_ref):
    o_ref[...] = _dot3(x_ref[...], sel_ref[...])


def _prefix(x, sel, *, tm):
    n, k = x.shape
    w = sel.shape[1]
    assert n % tm == 0
    return pl.pallas_call(
        _prefix_kernel,
        grid=(n // tm,),
        in_specs=[pl.BlockSpec((tm, k), lambda i: (i, 0)), pl.BlockSpec((k, w), lambda i: (0, 0))],
        out_specs=pl.BlockSpec((tm, w), lambda i: (i, 0)),
        out_shape=jax.ShapeDtypeStruct((n, w), F32),
        compiler_params=pltpu.CompilerParams(dimension_semantics=("arbitrary",)),
        name="logf_prefix",
    )(x, sel)


def _block_diag_queries(q, n_rep, row_shift, lane_shift):
    qt = jnp.concatenate([q] * n_rep, axis=0)
    rg = lax.broadcasted_iota(jnp.int32, qt.shape, 0) >> row_shift
    lg = lax.broadcasted_iota(jnp.int32, qt.shape, 1) >> lane_shift
    return jnp.where(rg == lg, qt, 0.0).astype(BF16)


def _pad_rows(x, rows):
    return jnp.concatenate([x, jnp.zeros((rows - x.shape[0], x.shape[1]), x.dtype)], axis=0)


def _expand_heads(pieces, n_dec):
    sub = lax.broadcasted_iota(jnp.int32, (2 * n_dec, LANES), 0)
    out = []
    for p in range(len(pieces) // 2):
        a = jnp.broadcast_to(pieces[2 * p], (2 * n_dec, LANES))
        b = jnp.broadcast_to(pieces[2 * p + 1], (2 * n_dec, LANES))
        out.append(jnp.where(sub < n_dec, a, b))
    return jnp.concatenate(out, axis=0)


def _sample_stick(q, k_refs, vt_refs, kn_ref, vn_ref, acc_ref, n_dec):
    n_pages = len(k_refs)
    n_rows = HA * n_dec
    qbd = _block_diag_queries(q, HA, 2, 6)
    rowi = lax.broadcasted_iota(jnp.int32, (n_rows, LANES), 0)
    lane = lax.broadcasted_iota(jnp.int32, (n_rows, LANES), 1)
    strict = lane < (rowi & (n_dec - 1))
    key, key2 = _tile_iotas(LANES)
    later = (key > key2).astype(BF16)
    zs = [_dot(qbd, k_refs[r][0, 0].astype(BF16)) for r in range(n_pages)]
    zs.append(_dot_nt(qbd, _pad_rows(kn_ref[0], LANES).astype(BF16)))
    ls = [_log2_sigmoid(z) for z in zs]
    lk = [l - z for l, z in zip(ls, zs)]
    lk[-1] = jnp.where(strict, lk[-1], 0.0)
    his = [t.astype(BF16) for t in lk]
    los = [(t - h.astype(F32)).astype(BF16) for t, h in zip(lk, his)]
    cum = _dot(jnp.concatenate(his + los, axis=0), later)
    n_t = n_pages + 1
    cums = [cum[i * n_rows:(i + 1) * n_rows] + cum[(n_t + i) * n_rows:(n_t + i + 1) * n_rows] for i in range(n_t)]
    sums = [jnp.sum(t, axis=1, keepdims=True) for t in lk]
    w = jnp.where(strict, jnp.exp2(ls[-1] + cums[-1]), 0.0)
    acc = _dot(w.astype(BF16), _pad_rows(vn_ref[0], LANES).astype(BF16))
    c = sums[-1]
    for r in range(n_pages - 1, -1, -1):
        w = jnp.exp2(ls[r] + cums[r] + c)
        acc = acc + _dot_nt(w.astype(BF16), vt_refs[r][0, 0].astype(BF16))
        c = c + sums[r]
    acc_ref[...] = acc


def _sample_softmax(scores, vt_refs, vn_ref, acc_ref, l_ref):
    m = jnp.max(functools.reduce(jnp.maximum, scores), axis=1, keepdims=True)
    ps = [jnp.exp2(s - m) for s in scores]
    l_ref[...] = jnp.sum(functools.reduce(jnp.add, ps), axis=1, keepdims=True)
    acc = _dot(ps[-1].astype(BF16), _pad_rows(vn_ref[0], LANES).astype(BF16))
    for r, vt_ref in enumerate(vt_refs):
        acc = acc + _dot_nt(ps[r].astype(BF16), vt_ref[0, 0].astype(BF16))
    acc_ref[...] = acc


def _sample_kernel(pt_ref, q_ref, *refs, n_dec, n_pages, lam_init):
    del pt_ref
    refs = list(refs)
    take = lambda n: [refs.pop(0) for _ in range(n)]
    ka, va, kb, vb, kc, vc, fp = (take(n_pages) for _ in range(7))
    kna, vna, knb, vnb, knc, vnc, fn_ref, dl_ref, dn_ref = take(9)
    oa_ref, ob_ref, oc_ref = take(3)
    acc_a, acc_b, l_b, acc_c, l_c = take(5)
    q = q_ref[0]

    _sample_stick(q[:, 0:WA], ka, va, kna, vna, acc_a, n_dec)

    n_rows = HB * n_dec
    qbd = _block_diag_queries(q[:, WA:WA + WB], HB, 2, 6)
    rowi = lax.broadcasted_iota(jnp.int32, (n_rows, LANES), 0)
    lane = lax.broadcasted_iota(jnp.int32, (n_rows, LANES), 1)
    qpos = rowi & (n_dec - 1)
    fnew = _expand_heads([fn_ref[0][:, h * LANES:(h + 1) * LANES] for h in range(HB)], n_dec)
    cs = jnp.sum(jnp.where(lane == qpos, fnew, 0.0), axis=1, keepdims=True)
    s_new = _dot_nt(qbd, _pad_rows(knb[0], LANES).astype(BF16)) + (cs - fnew) * LOG2E
    scores = [None] * n_pages + [jnp.where(lane <= qpos, s_new, NEG_INF)]
    for r in range(n_pages - 1, -1, -1):
        f = fp[r][0, 0]
        fpg = _expand_heads([f[h:h + 1, :] for h in range(HB)], n_dec)
        cs = cs + fpg[:, LANES - 1:LANES]
        scores[r] = _dot(qbd, kb[r][0, 0].astype(BF16)) + (cs - fpg) * LOG2E
    _sample_softmax(scores, vb, vnb, acc_b, l_b)

    n_rows = HC * 2 * n_dec
    qbd = _block_diag_queries(q[:, WA + WB:], HC * 2, 2, 5)
    rowi = lax.broadcasted_iota(jnp.int32, (n_rows, LANES), 0)
    lane = lax.broadcasted_iota(jnp.int32, (n_rows, LANES), 1)
    scores = [_dot(qbd, kc[r][0, 0].astype(BF16)) for r in range(n_pages)]
    scores.append(jnp.where(lane <= (rowi & (n_dec - 1)), _dot_nt(qbd, _pad_rows(knc[0], LANES).astype(BF16)), NEG_INF))
    _sample_softmax(scores, vc, vnc, acc_c, l_c)

    laneh = lax.broadcasted_iota(jnp.int32, (n_dec, WA), 1) >> 6
    out = jnp.zeros((n_dec, WA), F32)
    for h in range(HA):
        out = jnp.where(laneh == h, acc_a[h * n_dec:(h + 1) * n_dec, :], out)
    oa_ref[0] = out

    laneh = lax.broadcasted_iota(jnp.int32, (n_dec, WB), 1) >> 6
    out = jnp.zeros((n_dec, WB), F32)
    for h in range(HB):
        rows = slice(h * n_dec, (h + 1) * n_dec)
        out = jnp.where(laneh == h, acc_b[rows, :] / l_b[rows, :], out)
    ob_ref[0] = out

    lam = _diff_lambda(dl_ref, lam_init)
    out = jnp.zeros((n_dec, WC), F32)
    for h in range(HC):
        r1 = slice((2 * h) * n_dec, (2 * h + 1) * n_dec)
        r2 = slice((2 * h + 1) * n_dec, (2 * h + 2) * n_dec)
        o = acc_c[r1, :] / l_c[r1, :] - lam * (acc_c[r2, :] / l_c[r2, :])
        mine = laneh == h
        ms = jnp.sum(jnp.where(mine, o * o, 0.0), axis=1, keepdims=True) * (1.0 / HEAD_DIM)
        out = jnp.where(mine, o * lax.rsqrt(ms + EPS), out)
    oc_ref[0] = out * dn_ref[...] * (1.0 - lam_init)


def _sample_attn(page_table, q3, caches_t, fpage, new_rows, fnew, dl, dn, layer, *, lam_init):
    nb, n_dec, d = q3.shape
    assert 2 * n_dec == SUBLANES, "two heads' query rows must fill one 8-sublane tile"
    n_pages = page_table.shape[1]
    page = caches_t[0].shape[3]
    assert page == LANES
    per_b = lambda b, pt: (b, 0, 0)
    const = lambda b, pt: (0, 0)
    in_specs = [pl.BlockSpec((1, n_dec, d), per_b)]
    args = [q3]
    for c in tuple(caches_t) + (fpage,):
        for r in range(n_pages):
            in_specs.append(pl.BlockSpec((1, 1, c.shape[2], page), lambda b, pt, r=r: (layer, pt[b, r], 0, 0)))
            args.append(c)
    for t in new_rows:
        in_specs.append(pl.BlockSpec((1, n_dec, t.shape[2]), per_b))
        args.append(t)
    in_specs += [pl.BlockSpec((1, 1, HB * LANES), per_b), pl.BlockSpec((4, QK_C), const),
                 pl.BlockSpec((1, WC), const)]
    args += [fnew, dl, dn]
    widths = (WA, WB, WC)
    scratch = [pltpu.VMEM((HA * n_dec, WA), F32),
               pltpu.VMEM((HB * n_dec, WB), F32), pltpu.VMEM((HB * n_dec, 1), F32),
               pltpu.VMEM((HC * 2 * n_dec, WC), F32), pltpu.VMEM((HC * 2 * n_dec, 1), F32)]
    grid_spec = pltpu.PrefetchScalarGridSpec(
        num_scalar_prefetch=1,
        grid=(nb,),
        in_specs=in_specs,
        out_specs=tuple(pl.BlockSpec((1, n_dec, w), per_b) for w in widths),
        scratch_shapes=scratch,
    )
    return pl.pallas_call(
        functools.partial(_sample_kernel, n_dec=n_dec, n_pages=n_pages, lam_init=lam_init),
        grid_spec=grid_spec,
        out_shape=tuple(jax.ShapeDtypeStruct((nb, n_dec, w), F32) for w in widths),
        compiler_params=pltpu.CompilerParams(dimension_semantics=("arbitrary",), vmem_limit_bytes=VMEM_LIMIT),
        name="sample_attn",
    )(page_table, *args)


def _rotate_half_columns(w):
    lead = w.shape[:-1]
    g = w.reshape(lead + (w.shape[-1] // QK_C, 2, QK_C // 2))
    return jnp.concatenate([-g[..., 1:2, :], g[..., 0:1, :]], axis=-2).reshape(w.shape)


def _arrange_w_in(w_in):
    sizes = [WA] * 3 + [WB] * 3 + [HB] + [WC] * 3
    bounds = [0]
    for s in sizes:
        bounds.append(bounds[-1] + s)
    qa, ka, va, qb, kb, vb, fb, qc, kc, vc = [w_in[..., bounds[i]:bounds[i + 1]] for i in range(len(sizes))]
    pad = jnp.zeros(w_in.shape[:-1] + (LANES - HB,), w_in.dtype)
    cols = [qa, ka, va, qb, kb, vb, qc, kc, vc, _rotate_half_columns(qc), _rotate_half_columns(kc), fb, pad]
    w_arr = jnp.concatenate(cols, axis=-1).astype(BF16)
    w_vt = jnp.swapaxes(jnp.concatenate([va, vb, vc], axis=-1), -1, -2).astype(BF16)
    return w_arr, w_vt


def _rope_tables(pos):
    half = QK_C // 2
    inv = ROPE_THETA ** (-jnp.arange(half, dtype=F32) / half)
    ang = pos.astype(F32)[:, None] * inv[None, :]
    reps = WC // half
    return jnp.tile(jnp.cos(ang), (1, reps)), jnp.tile(jnp.sin(ang), (1, reps))


def _new_prefix_matrix(n_dec, n_heads):
    r = jnp.arange(n_dec * LANES)
    c = jnp.arange(n_heads * LANES)
    m = (((r % LANES)[:, None] == (c // LANES)[None, :]) & ((r // LANES)[:, None] <= (c % LANES)[None, :])
         & ((c % LANES)[None, :] < n_dec))
    return m.astype(BF16)


def _keys_on_lanes(cache):
    l, pool, page, h, hd = cache.shape
    return jnp.transpose(cache, (0, 1, 3, 4, 2)).reshape(l, pool, h * hd, page)


def kernel(x_prompt, x_sample, cache_a_k, cache_a_v, cache_b_k, cache_b_v, cache_b_logf, cache_c_k, cache_c_v,
           page_table, w_in, b_forget, w_out, g_attn, diff_lambda, diff_norm, g_ffn, w_group, b_group,
           w_expert_router, b_expert_router, w_up_gate, w_up, w_down, g_final):
    nb, nt, d = x_prompt.shape
    sb, n_dec, _ = x_sample.shape
    depth = w_in.shape[0]
    n_pool, page = cache_a_k.shape[1], cache_a_k.shape[2]
    n_pages = page_table.shape[1]
    assert page == LANES
    tm = min(512, nt)
    tq = min(256, nt)
    tq_b = min(512, nt)

    w_arr, w_vt = _arrange_w_in(w_in)
    bf_pad = jnp.pad(b_forget.astype(F32), ((0, 0), (0, LANES - HB)))[:, None, :]
    wo = w_out.astype(BF16)
    wr_t = jnp.zeros((depth, _R_ROWS, d), F32)
    wr_t = wr_t.at[:, 0:N_GROUPS].set(jnp.swapaxes(w_group, 1, 2).astype(F32))
    wr_t = wr_t.at[:, _R_EXP0:_R_EXP0 + N_GROUPS * EXPERTS_PER_GROUP].set(
        jnp.swapaxes(w_expert_router, 1, 2).astype(F32))
    wr_hi = wr_t.astype(BF16)
    wr_t = jnp.concatenate([wr_hi, (wr_t - wr_hi.astype(F32)).astype(BF16)], axis=1)
    br = jnp.zeros((depth, _R_ROWS), F32)
    br = br.at[:, 0:N_GROUPS].set(b_group.astype(F32))
    br = br.at[:, _R_EXP0:_R_EXP0 + N_GROUPS * EXPERTS_PER_GROUP].set(b_expert_router.astype(F32))[:, :, None]
    w1 = w_up_gate.astype(BF16)
    w3 = w_up.astype(BF16)
    w2 = w_down.astype(BF16)
    dn_col = diff_norm.astype(F32)[:, :, None]
    dn_row = jnp.tile(diff_norm.astype(F32), (1, HC))[:, None, :]
    dl = diff_lambda.astype(F32)

    cos_p, sin_p = _rope_tables(jnp.arange(nt))
    pos_s = n_pages * page + jnp.arange(n_dec)
    cos_s, sin_s = _rope_tables(jnp.tile(pos_s, sb))

    caches_t = tuple(_keys_on_lanes(c.astype(F32)) for c in
                     (cache_a_k, cache_a_v, cache_b_k, cache_b_v, cache_c_k, cache_c_v))
    lf_rows = jnp.transpose(cache_b_logf.astype(F32), (0, 3, 1, 2)).reshape(depth * HB * n_pool, page)
    key, key2 = jnp.arange(page)[:, None], jnp.arange(page)[None, :]
    fpage = _prefix(lf_rows, (key <= key2).astype(BF16), tm=math.gcd(depth * HB * n_pool, 4096))
    fpage = jnp.transpose(fpage.reshape(depth, HB, n_pool, page), (0, 2, 1, 3))
    sel_new = _new_prefix_matrix(n_dec, HB)

    xp = x_prompt.reshape(nb * nt, d).astype(F32)
    xs = x_sample.reshape(sb * n_dec, d).astype(F32)
    ns = sb * n_dec
    rows_p = None
    rows_s = [[] for _ in range(7)]
    for l in range(depth):
        lam_init = 0.8 - 0.6 * math.exp(-0.3 * l)
        g_a = g_attn[l].astype(F32)[None, :]
        g_f = g_ffn[l].astype(F32)[None, :]

        outs = _inproj(xp, g_a, w_arr[l], w_vt[l], bf_pad[l], cos_p, sin_p, tm=tm, n_pos_tiles=nt // tm,
                       n_batch=nb, layer=l, depth=depth, stacked=rows_p)
        q, lfp = outs[0], outs[6]
        rows_p = tuple(outs[i] for i in _STACKED)
        ka, vat, kb, vbt, _, kc, vct = rows_p
        b3 = lambda t: t.reshape(nb, nt, t.shape[-1])
        b4 = lambda t: t.reshape(depth, nb, nt, t.shape[-1])
        q3 = b3(q)
        oa = _prompt_attn("a", q3, b4(ka), vat, (), l, tq=tq)
        ob = _prompt_attn("b", q3, b4(kb), vbt, (b3(lfp),), l, tq=tq_b)
        oc = _prompt_attn("c", q3, b4(kc), vct, (dl[l], dn_col[l]), l, tq=tq, lam_init=lam_init)
        flat = lambda t: t.reshape(nb * nt, t.shape[-1])
        xp = _outproj_moe(xp, flat(oa), flat(ob), flat(oc), wo[l], g_f, wr_t[l], br[l], w1, w3, w2, l, tm=tm)

        q, ka, va, kb, vb, lf, lfp, kc, vc = _inproj(xs, g_a, w_arr[l], None, bf_pad[l], cos_s, sin_s,
                                                     tm=ns, n_pos_tiles=1)
        s3 = lambda t: t.reshape(sb, n_dec, t.shape[-1])
        fnew = _prefix(lfp.reshape(sb, n_dec * LANES), sel_new, tm=sb).reshape(sb, 1, HB * LANES)
        oa, ob, oc = _sample_attn(page_table, s3(q.astype(F32)), caches_t, fpage,
                                  tuple(s3(t) for t in (ka, va, kb, vb, kc, vc)), fnew, dl[l], dn_row[l], l,
                                  lam_init=lam_init)
        sflat = lambda t: t.reshape(ns, t.shape[-1]).astype(BF16)
        xs = _outproj_moe(xs, sflat(oa), sflat(ob), sflat(oc), wo[l], g_f, wr_t[l], br[l], w1, w3, w2, l, tm=ns)
        for i, t in enumerate((ka, va, kb, vb, lf, kc, vc)):
            rows_s[i].append(t)

    g_fin = g_final.astype(F32)[None, :]
    y_prompt = _final_norm(xp, g_fin, tm=tm).reshape(nb, nt, d)
    y_sample = _final_norm(xs, g_fin, tm=ns).reshape(sb, n_dec, d)

    def heads(arrays, b, t, transposed):
        outs = []
        for i, a in enumerate(arrays):
            if i == 4:
                outs.append(a.reshape(depth, b, t, HB))
            elif transposed and i in (1, 3, 6):
                a = a.reshape(depth, b, a.shape[2] // HEAD_DIM, HEAD_DIM, t)
                outs.append(jnp.transpose(a, (0, 1, 4, 2, 3)))
            else:
                outs.append(a.reshape(depth, b, t, a.shape[-1] // HEAD_DIM, HEAD_DIM))
        return outs

    rows_s = [jnp.stack(r, axis=0) for r in rows_s]
    return (y_prompt, y_sample, *heads(rows_p, nb, nt, True), *heads(rows_s, sb, n_dec, False))
```

```python
import functools
import math

import jax
import jax.numpy as jnp
from jax import lax
from jax.experimental import pallas as pl
from jax.experimental.pallas import tpu as pltpu

F32 = jnp.float32
BF16 = jnp.bfloat16

HEAD_DIM = 64
HA, HB, HC = 4, 6, 6
QK_C = HEAD_DIM // 2
WA, WB, WC = HA * HEAD_DIM, HB * HEAD_DIM, HC * HEAD_DIM
ROPE_THETA = 10000.0
N_GROUPS = 4
EXPERTS_PER_GROUP = 4
EPS = 1e-6
LOG2E = 1.4426950408889634
LANES = 128
SUBLANES = 8
HEADS_PER_BLOCK = LANES // HEAD_DIM
VMEM_LIMIT = 56 * 1024 * 1024
NEG_INF = float("-inf")

_C_QA, _C_KA, _C_VA = 0, WA, 2 * WA
_C_QB = 3 * WA
_C_KB, _C_VB = _C_QB + WB, _C_QB + 2 * WB
_C_QC = _C_QB + 3 * WB
_C_KC, _C_VC = _C_QC + WC, _C_QC + 2 * WC
_C_QCR, _C_KCR = _C_QC + 3 * WC, _C_QC + 4 * WC
_C_F = _C_QC + 5 * WC
_P_ARR = _C_F + LANES


def _dot(a, b):
    return jnp.dot(a, b, preferred_element_type=F32)


def _dot_nt(a, b):
    return lax.dot_general(a, b, (((1,), (1,)), ((), ())), preferred_element_type=F32)


def _log_sigmoid(z):
    return jnp.minimum(z, 0.0) - jnp.log(1.0 + jnp.exp(-jnp.abs(z)))


def _log2_sigmoid(z2):
    return jnp.minimum(z2, 0.0) - jnp.log2(1.0 + jnp.exp2(-jnp.abs(z2)))


def _split3(x):
    hi = x.astype(BF16)
    r = x - hi.astype(F32)
    mid = r.astype(BF16)
    lo = (r - mid.astype(F32)).astype(BF16)
    return hi, mid, lo


def _dot3(x, m):
    hi, mid, lo = _split3(x)
    return _dot(hi, m) + _dot(mid, m) + _dot(lo, m)


def _cumsum_rows(x, incl):
    hi, mid, lo = _split3(x)
    return _dot(incl, hi) + _dot(incl, mid) + _dot(incl, lo)


def _inproj_common(x_ref, g_ref, w_ref, bf_ref, cos_ref, sin_ref, q_ref, ka_ref, kb_ref, lf_ref, lfp_ref, kc_ref):
    x = x_ref[...]
    ms = jnp.mean(x * x, axis=-1, keepdims=True)
    h = (x * lax.rsqrt(ms + EPS) * g_ref[...]).astype(BF16)

    def proj(c0, width):
        return _dot(h, w_ref[:, c0:c0 + width])

    q_ref[:, 0:WA] = (proj(_C_QA, WA) * (HEAD_DIM ** -0.5 * LOG2E)).astype(BF16)
    ka_ref[...] = proj(_C_KA, WA)
    q_ref[:, WA:WA + WB] = (proj(_C_QB, WB) * (HEAD_DIM ** -0.5 * LOG2E)).astype(BF16)
    kb_ref[...] = proj(_C_KB, WB)
    cos = cos_ref[...]
    sin = sin_ref[...]
    qc = proj(_C_QC, WC) * cos + proj(_C_QCR, WC) * sin
    q_ref[:, WA + WB:] = (qc * (QK_C ** -0.5 * LOG2E)).astype(BF16)
    kc_ref[...] = proj(_C_KC, WC) * cos + proj(_C_KCR, WC) * sin
    lf = _log_sigmoid(proj(_C_F, LANES) + bf_ref[...])
    lane = lax.broadcasted_iota(jnp.int32, lf.shape, 1)
    lfp_ref[...] = jnp.where(lane < HB, lf, 0.0)
    lf_ref[...] = lf[:, :HB]
    return h, proj


def _inproj_rows_kernel(x_ref, g_ref, w_ref, bf_ref, cos_ref, sin_ref,
                        q_ref, ka_ref, va_ref, kb_ref, vb_ref, lf_ref, lfp_ref, kc_ref, vc_ref):
    _, proj = _inproj_common(x_ref, g_ref, w_ref, bf_ref, cos_ref, sin_ref,
                             q_ref, ka_ref, kb_ref, lf_ref, lfp_ref, kc_ref)
    va_ref[...] = proj(_C_VA, WA)
    vb_ref[...] = proj(_C_VB, WB)
    vc_ref[...] = proj(_C_VC, WC)


def _inproj_vt_kernel(x_ref, g_ref, w_ref, wvt_ref, bf_ref, cos_ref, sin_ref, *refs):
    q_ref, ka_ref, vat_ref, kb_ref, vbt_ref, lf_ref, lfp_ref, kc_ref, vct_ref = refs[-9:]
    h, _ = _inproj_common(x_ref, g_ref, w_ref, bf_ref, cos_ref, sin_ref,
                          q_ref, ka_ref, kb_ref, lf_ref, lfp_ref, kc_ref)
    vat_ref[0] = _dot_nt(wvt_ref[0:WA, :], h)
    vbt_ref[0] = _dot_nt(wvt_ref[WA:WA + WB, :], h)
    vct_ref[0] = _dot_nt(wvt_ref[WA + WB:, :], h)


_STACKED = (1, 2, 3, 4, 5, 7, 8)


def _inproj(x2, g, w_arr, w_vt, bf_pad, cos_t, sin_t, *, tm, n_pos_tiles, n_batch=None, layer=0, depth=1,
            stacked=None):
    n, d = x2.shape
    assert n % tm == 0
    grid = (n // tm,)
    row = lambda i: (i, 0)
    const = lambda i: (0, 0)
    pos = lambda i: (i % n_pos_tiles, 0)
    rows = lambda w, dt=F32: (jax.ShapeDtypeStruct((n, w), dt), pl.BlockSpec((tm, w), row))
    in_specs = [pl.BlockSpec((tm, d), row), pl.BlockSpec((1, d), const), pl.BlockSpec((d, _P_ARR), const)]
    args = [x2, g, w_arr]
    if w_vt is None:
        body = _inproj_rows_kernel
        outs = [rows(WA + WB + WC, BF16), rows(WA), rows(WA), rows(WB), rows(WB), rows(HB), rows(LANES), rows(WC),
                rows(WC)]
    else:
        body = _inproj_vt_kernel
        in_specs.append(pl.BlockSpec(w_vt.shape, const))
        args.append(w_vt)
        nt = n // n_batch
        srow = lambda w: (jax.ShapeDtypeStruct((depth, n, w), F32),
                          pl.BlockSpec((None, tm, w), lambda i: (layer, i, 0)))
        sval = lambda w: (jax.ShapeDtypeStruct((depth, n_batch, w, nt), F32),
                          pl.BlockSpec((None, 1, w, tm), lambda i: (layer, i // n_pos_tiles, 0, i % n_pos_tiles)))
        outs = [rows(WA + WB + WC, BF16), srow(WA), sval(WA), srow(WB), sval(WB), srow(HB), rows(LANES), srow(WC),
                sval(WC)]
    in_specs += [pl.BlockSpec((1, LANES), const), pl.BlockSpec((tm, WC), pos), pl.BlockSpec((tm, WC), pos)]
    args += [bf_pad, cos_t, sin_t]
    aliases = {}
    if stacked is not None:
        for arr, out_idx in zip(stacked, _STACKED):
            aliases[len(args)] = out_idx
            in_specs.append(pl.BlockSpec(memory_space=pl.ANY))
            args.append(arr)
    return pl.pallas_call(
        body,
        grid=grid,
        in_specs=in_specs,
        out_specs=tuple(o[1] for o in outs),
        out_shape=tuple(o[0] for o in outs),
        input_output_aliases=aliases,
        compiler_params=pltpu.CompilerParams(dimension_semantics=("arbitrary",), vmem_limit_bytes=VMEM_LIMIT),
        name="inproj",
    )(*args)


def _tile_iotas(t):
    return lax.broadcasted_iota(jnp.int32, (t, t), 0), lax.broadcasted_iota(jnp.int32, (t, t), 1)


def _masked_queries(q, width):
    lane = lax.broadcasted_iota(jnp.int32, q.shape, 1)
    zero = jnp.zeros_like(q)
    return jnp.concatenate([jnp.where((lane >= g * width) & (lane < (g + 1) * width), q, zero)
                            for g in range(LANES // width)], axis=0)


def _cast_kv(qi, k_ref, vt_ref, kb_ref, vtb_ref):
    @pl.when(qi == 0)
    def _():
        kb_ref[...] = k_ref[0].astype(BF16)
        vtb_ref[...] = vt_ref[0].astype(BF16)


def _sweep(qi, tq, kb_ref, qs, s_ref, update, init):
    def scores(kj):
        kt = kb_ref[pl.ds(pl.multiple_of(kj * tq, tq), tq), :]
        half = qs.shape[0] // 2
        return jnp.concatenate([_dot_nt(kt, qs[:half]), _dot_nt(kt, qs[half:])], axis=1)

    odd = qi % 2
    first = qi - 1 - odd

    def prefetch_first():
        s_ref[...] = scores(jnp.maximum(first, 0))

    state = lax.cond(
        odd == 1,
        lambda: update([qi, qi - 1], [scores(qi), scores(jnp.maximum(qi - 1, 0))], init, True, prefetch_first),
        lambda: update([qi], [scores(qi)], init, True, prefetch_first))

    def body(it, st):
        kj = first - 2 * it
        s_a = s_ref[...]
        s_b = scores(kj - 1)

        def prefetch():
            s_ref[...] = scores(jnp.maximum(kj - 2, 0))

        return update([kj, kj - 1], [s_a, s_b], st, False, prefetch)

    return lax.fori_loop(0, qi // 2, body, state)


def _attn_a_kernel(q_ref, k_ref, vt_ref, o_ref, kb_ref, vtb_ref, s_ref, *, tq):
    qi = pl.program_id(2)
    _cast_kv(qi, k_ref, vt_ref, kb_ref, vtb_ref)
    qs = _masked_queries(q_ref[0], HEAD_DIM)
    key, qry = _tile_iotas(tq)
    strict = key < qry
    later = (qry > key).astype(BF16)

    def tiles(kjs, z_alls, carry, diag, prefetch):
        n_g = len(kjs) * HEADS_PER_BLOCK
        zs = [z_all[:, hh * tq:(hh + 1) * tq] for z_all in z_alls for hh in range(HEADS_PER_BLOCK)]
        ls = [_log2_sigmoid(z) for z in zs]
        lk = [l - z for l, z in zip(ls, zs)]
        n_masked = HEADS_PER_BLOCK if diag else 0
        lk = [jnp.where(strict, t, 0.0) if g < n_masked else t for g, t in enumerate(lk)]
        prefetch()
        cum = _dot(later, jnp.concatenate([t.astype(BF16) for t in lk], axis=1))
        ws = [jnp.exp2(ls[g] + cum[:, g * tq:(g + 1) * tq]) for g in range(n_g)]
        ws = [jnp.where(strict, w, 0.0) if g < n_masked else w for g, w in enumerate(ws)]
        new = list(carry)
        for t, kj in enumerate(kjs):
            vt = vtb_ref[:, pl.ds(pl.multiple_of(kj * tq, tq), tq)]
            for hh in range(HEADS_PER_BLOCK):
                g = t * HEADS_PER_BLOCK + hh
                c, acc = new[hh]
                acc = acc + _dot(vt[hh * HEAD_DIM:(hh + 1) * HEAD_DIM, :], ws[g].astype(BF16)) * jnp.exp2(c)
                new[hh] = (c + jnp.sum(lk[g], axis=0, keepdims=True), acc)
        return tuple(new)

    init = tuple((jnp.zeros((1, tq), F32), jnp.zeros((HEAD_DIM, tq), F32)) for _ in range(HEADS_PER_BLOCK))
    carry = _sweep(qi, tq, kb_ref, qs, s_ref, tiles, init)
    o_t = jnp.concatenate([carry[hh][1] for hh in range(HEADS_PER_BLOCK)], axis=0)
    o_ref[0] = o_t.T.astype(BF16)


def _softmax_step(s, state, vt_rows, masked, causal):
    m, l, acc = state
    if masked:
        s = jnp.where(causal, s, NEG_INF)
    m_new = jnp.maximum(m, jnp.max(s, axis=0, keepdims=True))
    alpha = jnp.exp2(m - m_new)
    p = jnp.exp2(s - m_new)
    l = alpha * l + jnp.sum(p, axis=0, keepdims=True)
    acc = alpha * acc + _dot(vt_rows, p.astype(BF16))
    return m_new, l, acc


def _softmax_init(tq):
    return (jnp.full((1, tq), NEG_INF, F32), jnp.zeros((1, tq), F32), jnp.zeros((HEAD_DIM, tq), F32))


_BIAS_LANES = 8


def _attn_b_kernel(q_ref, k_ref, vt_ref, lf_ref, o_ref, kb_ref, vtb_ref, s_ref, qb_ref, *, tq, n_t):
    pair = pl.program_id(1)
    qi = pl.program_id(2)
    key, qry = _tile_iotas(tq)

    @pl.when(qi == 0)
    def _():
        kb_ref[:, 0:LANES] = k_ref[0].astype(BF16)
        vtb_ref[...] = vt_ref[0].astype(BF16)
        incl = (qry <= key).astype(BF16)
        src, dst = _tile_iotas(LANES)
        lane = lax.broadcasted_iota(jnp.int32, (tq, LANES), 1)
        sel_k, sel_q = [], []
        ones_k = jnp.zeros((tq, LANES), F32)
        neg_q = jnp.zeros((tq, LANES), F32)
        for part in range(3):
            mk = mq = None
            for hh in range(HEADS_PER_BLOCK):
                hg = pair * HEADS_PER_BLOCK + hh
                k_lane = hh * _BIAS_LANES + part
                q_lane = hh * _BIAS_LANES + 3 + part
                a = (src == hg) & (dst == k_lane)
                b = (src == hg) & (dst == q_lane)
                mk = a if mk is None else mk | a
                mq = b if mq is None else mq | b
                ones_k = jnp.where(lane == q_lane, 1.0, ones_k)
                neg_q = jnp.where(lane == k_lane, -1.0, neg_q)
            sel_k.append(mk.astype(BF16))
            sel_q.append(mq.astype(BF16))
        carry = jnp.zeros((1, LANES), F32)
        for c in range(n_t // tq):
            f = _cumsum_rows(lf_ref[0, c * tq:(c + 1) * tq, :], incl) + carry
            carry = f[tq - 1:tq, :]
            parts = _split3(f * LOG2E)
            k_bias = ones_k + functools.reduce(jnp.add, [_dot(p, m) for p, m in zip(parts, sel_k)])
            q_bias = neg_q + functools.reduce(jnp.add, [_dot(p, m) for p, m in zip(parts, sel_q)])
            kb_ref[c * tq:(c + 1) * tq, LANES:] = k_bias.astype(BF16)
            qb_ref[c * tq:(c + 1) * tq, :] = q_bias.astype(BF16)

    q = q_ref[0]
    qb = qb_ref[pl.ds(pl.multiple_of(qi * tq, tq), tq), :]
    lane = lax.broadcasted_iota(jnp.int32, q.shape, 1)
    zero = jnp.zeros_like(q)
    qs = jnp.concatenate(
        [jnp.concatenate([jnp.where((lane >= hh * HEAD_DIM) & (lane < (hh + 1) * HEAD_DIM), q, zero),
                          jnp.where((lane >= hh * _BIAS_LANES) & (lane < (hh + 1) * _BIAS_LANES), qb, zero)],
                         axis=1) for hh in range(HEADS_PER_BLOCK)], axis=0)
    causal = key <= qry

    def tiles(kjs, s_alls, carry, diag, prefetch):
        for t, (kj, s_all) in enumerate(zip(kjs, s_alls)):
            if t == len(kjs) - 1:
                prefetch()
            vt = vtb_ref[:, pl.ds(pl.multiple_of(kj * tq, tq), tq)]
            new = []
            for hh in range(HEADS_PER_BLOCK):
                s = s_all[:, hh * tq:(hh + 1) * tq]
                new.append(_softmax_step(s, carry[hh], vt[hh * HEAD_DIM:(hh + 1) * HEAD_DIM, :],
                                         diag and t == 0, causal))
            carry = tuple(new)
        return carry

    carry = _sweep(qi, tq, kb_ref, qs, s_ref, tiles, tuple(_softmax_init(tq) for _ in range(HEADS_PER_BLOCK)))
    o_t = jnp.concatenate([carry[hh][2] / carry[hh][1] for hh in range(HEADS_PER_BLOCK)], axis=0)
    o_ref[0] = o_t.T.astype(BF16)


def _diff_lambda(dl_ref, lam_init):
    dl = dl_ref[...]
    a = jnp.sum(dl[0:1, :] * dl[1:2, :], axis=1, keepdims=True)
    b = jnp.sum(dl[2:3, :] * dl[3:4, :], axis=1, keepdims=True)
    return jnp.exp(a) - jnp.exp(b) + lam_init


def _attn_c_kernel(q_ref, k_ref, vt_ref, dl_ref, dn_ref, o_ref, kb_ref, vtb_ref, s_ref, *, tq, lam_init):
    qi = pl.program_id(2)
    _cast_kv(qi, k_ref, vt_ref, kb_ref, vtb_ref)
    qs = _masked_queries(q_ref[0], QK_C)
    n_maps = LANES // QK_C
    key, qry = _tile_iotas(tq)
    causal = key <= qry

    def tiles(kjs, s_alls, carry, diag, prefetch):
        for t, (kj, s_all) in enumerate(zip(kjs, s_alls)):
            if t == len(kjs) - 1:
                prefetch()
            vt = vtb_ref[:, pl.ds(pl.multiple_of(kj * tq, tq), tq)]
            new = []
            for mp in range(n_maps):
                hh = mp // 2
                s = s_all[:, mp * tq:(mp + 1) * tq]
                new.append(_softmax_step(s, carry[mp], vt[hh * HEAD_DIM:(hh + 1) * HEAD_DIM, :],
                                         diag and t == 0, causal))
            carry = tuple(new)
        return carry

    carry = _sweep(qi, tq, kb_ref, qs, s_ref, tiles, tuple(_softmax_init(tq) for _ in range(n_maps)))
    lam = _diff_lambda(dl_ref, lam_init)
    outs = []
    for hh in range(HEADS_PER_BLOCK):
        m1, m2 = carry[2 * hh], carry[2 * hh + 1]
        o = m1[2] / m1[1] - lam * (m2[2] / m2[1])
        ms = jnp.mean(o * o, axis=0, keepdims=True)
        outs.append(o * lax.rsqrt(ms + EPS) * dn_ref[...] * (1.0 - lam_init))
    o_ref[0] = jnp.concatenate(outs, axis=0).T.astype(BF16)


def _prompt_attn(kind, q_all, k, vt, extra, layer, *, tq, lam_init=None):
    _, nb, nt, w = k.shape
    assert nt % tq == 0 and tq % LANES == 0
    n_pairs = w // LANES
    q_blk0 = {"a": 0, "b": WA // LANES, "c": (WA + WB) // LANES}[kind]
    grid = (nb, n_pairs, nt // tq)
    q_spec = pl.BlockSpec((1, tq, LANES), lambda b, p, i: (b, i, q_blk0 + p))
    k_spec = pl.BlockSpec((None, 1, nt, LANES), lambda b, p, i: (layer, b, 0, p))
    vt_spec = pl.BlockSpec((None, 1, LANES, nt), lambda b, p, i: (layer, b, p, 0))
    o_spec = pl.BlockSpec((1, tq, LANES), lambda b, p, i: (b, i, p))
    n_groups = LANES // (QK_C if kind == "c" else HEAD_DIM)
    k_width = 2 * LANES if kind == "b" else LANES
    scratch = [pltpu.VMEM((nt, k_width), BF16), pltpu.VMEM((LANES, nt), BF16), pltpu.VMEM((tq, n_groups * tq), F32)]
    in_specs = [q_spec, k_spec, vt_spec]
    if kind == "a":
        body = functools.partial(_attn_a_kernel, tq=tq)
    elif kind == "b":
        body = functools.partial(_attn_b_kernel, tq=tq, n_t=nt)
        in_specs.append(pl.BlockSpec((1, nt, LANES), lambda b, p, i: (b, 0, 0)))
        scratch.append(pltpu.VMEM((nt, LANES), BF16))
    else:
        body = functools.partial(_attn_c_kernel, tq=tq, lam_init=lam_init)
        in_specs += [pl.BlockSpec((4, QK_C), lambda b, p, i: (0, 0)),
                     pl.BlockSpec((HEAD_DIM, 1), lambda b, p, i: (0, 0))]
    return pl.pallas_call(
        body,
        grid=grid,
        in_specs=in_specs,
        out_specs=o_spec,
        out_shape=jax.ShapeDtypeStruct((nb, nt, w), BF16),
        scratch_shapes=scratch,
        compiler_params=pltpu.CompilerParams(dimension_semantics=("arbitrary", "arbitrary", "arbitrary"),
                                             vmem_limit_bytes=VMEM_LIMIT),
        name="prompt_attn_" + kind,
    )(q_all, k, vt, *extra)


_R_EXP0 = 8
_R_ROWS = 32


def _router_rows(lt):
    g = [lt[j:j + 1, :] for j in range(N_GROUPS)]
    gmax = functools.reduce(jnp.maximum, g)
    idx = jnp.full(gmax.shape, N_GROUPS - 1, jnp.int32)
    for j in range(N_GROUPS - 2, -1, -1):
        idx = jnp.where(g[j] == gmax, j, idx)
    g_top = 1.0 / functools.reduce(jnp.add, [jnp.exp(gj - gmax) for gj in g])
    e_in = []
    for j in range(EXPERTS_PER_GROUP):
        r = _R_EXP0 + (N_GROUPS - 1) * EXPERTS_PER_GROUP + j
        ej = lt[r:r + 1, :]
        for gg in range(N_GROUPS - 2, -1, -1):
            r = _R_EXP0 + gg * EXPERTS_PER_GROUP + j
            ej = jnp.where(idx == gg, lt[r:r + 1, :], ej)
        e_in.append(ej)
    emax = functools.reduce(jnp.maximum, e_in)
    u = [jnp.exp(ej - emax) for ej in e_in]
    usum = functools.reduce(jnp.add, u)
    p = [uj / usum for uj in u]
    top1 = functools.reduce(jnp.maximum, p)
    i1 = jnp.full(top1.shape, EXPERTS_PER_GROUP - 1, jnp.int32)
    for j in range(EXPERTS_PER_GROUP - 2, -1, -1):
        i1 = jnp.where(p[j] == top1, j, i1)
    rest = [jnp.where(i1 == j, -1.0, p[j]) for j in range(EXPERTS_PER_GROUP)]
    top2 = functools.reduce(jnp.maximum, rest)
    i2 = jnp.full(top2.shape, EXPERTS_PER_GROUP - 1, jnp.int32)
    for j in range(EXPERTS_PER_GROUP - 2, -1, -1):
        i2 = jnp.where(rest[j] == top2, j, i2)
    denom = top1 + top2
    gw = [jnp.where((i1 == j) | (i2 == j), p[j] / denom * g_top, 0.0) for j in range(EXPERTS_PER_GROUP)]
    return gw, idx


def _outproj_moe_kernel(x_ref, oa_ref, ob_ref, oc_ref, wo_ref, gf_ref, wr_ref, br_ref, w1_ref, w3_ref, w2_ref,
                        out_ref, xmid_ref, h2_ref, gate_ref, y_ref):
    g = pl.program_id(1)
    tm = x_ref.shape[0]

    @pl.when(g == 0)
    def _():
        xm = (x_ref[...] + _dot(oa_ref[...], wo_ref[0:WA, :]) + _dot(ob_ref[...], wo_ref[WA:WA + WB, :])
              + _dot(oc_ref[...], wo_ref[WA + WB:, :]))
        xmid_ref[...] = xm
        ms = jnp.mean(xm * xm, axis=-1, keepdims=True)
        h2 = xm * lax.rsqrt(ms + EPS) * gf_ref[...]
        h2_hi = h2.astype(BF16)
        h2_ref[...] = h2_hi
        h2_lo = (h2 - h2_hi.astype(F32)).astype(BF16)
        part = _dot_nt(wr_ref[...], h2_hi)
        lt = (part[0:_R_ROWS] + part[_R_ROWS:] + _dot_nt(wr_ref[0:_R_ROWS, :], h2_lo)) + br_ref[...]
        gw, idx = _router_rows(lt)
        sub = lax.broadcasted_iota(jnp.int32, (SUBLANES, tm), 0)
        rows = jnp.where(sub == EXPERTS_PER_GROUP, idx.astype(F32), 0.0)
        for j in range(EXPERTS_PER_GROUP):
            rows = jnp.where(sub == j, gw[j], rows)
        full = jnp.concatenate([rows, jnp.zeros((LANES - SUBLANES, tm), F32)], axis=0)
        gate_ref[...] = full.T
        y_ref[...] = jnp.zeros_like(y_ref)

    h2b = h2_ref[...]
    gate = gate_ref[...]
    mine = gate[:, EXPERTS_PER_GROUP:EXPERTS_PER_GROUP + 1] == g.astype(F32)
    y = y_ref[...]
    for e in range(EXPERTS_PER_GROUP):
        a = _dot(h2b, w1_ref[0, e])
        b = _dot(h2b, w3_ref[0, e])
        gcol = jnp.where(mine, gate[:, e:e + 1], 0.0)
        hid = a * (1.0 / (1.0 + jnp.exp(-a))) * b * gcol
        y = y + _dot(hid.astype(BF16), w2_ref[0, e])
    y_ref[...] = y

    @pl.when(g == N_GROUPS - 1)
    def _():
        out_ref[...] = xmid_ref[...] + y_ref[...]


def _outproj_moe(x2, oa, ob, oc, wo, gf, wr_t, br, w1, w3, w2, layer, *, tm):
    n, d = x2.shape
    assert n % tm == 0
    f = w1.shape[-1]
    grid = (n // tm, N_GROUPS)
    row = lambda i, g: (i, 0)
    const = lambda i, g: (0, 0)
    wsel = lambda i, g: (layer, g, 0, 0)
    return pl.pallas_call(
        _outproj_moe_kernel,
        grid=grid,
        in_specs=[
            pl.BlockSpec((tm, d), row),
            pl.BlockSpec((tm, WA), row), pl.BlockSpec((tm, WB), row), pl.BlockSpec((tm, WC), row),
            pl.BlockSpec((d, d), const),
            pl.BlockSpec((1, d), const),
            pl.BlockSpec((2 * _R_ROWS, d), const),
            pl.BlockSpec((_R_ROWS, 1), const),
            pl.BlockSpec((1, EXPERTS_PER_GROUP, d, f), wsel),
            pl.BlockSpec((1, EXPERTS_PER_GROUP, d, f), wsel),
            pl.BlockSpec((1, EXPERTS_PER_GROUP, f, d), wsel),
        ],
        out_specs=pl.BlockSpec((tm, d), row),
        out_shape=jax.ShapeDtypeStruct((n, d), F32),
        scratch_shapes=[pltpu.VMEM((tm, d), F32), pltpu.VMEM((tm, d), BF16), pltpu.VMEM((tm, LANES), F32),
                        pltpu.VMEM((tm, d), F32)],
        compiler_params=pltpu.CompilerParams(dimension_semantics=("arbitrary", "arbitrary"),
                                             vmem_limit_bytes=VMEM_LIMIT),
        name="outproj_moe",
    )(x2, oa, ob, oc, wo, gf, wr_t, br, w1, w3, w2)


def _final_norm_kernel(x_ref, g_ref, o_ref):
    x = x_ref[...]
    ms = jnp.mean(x * x, axis=-1, keepdims=True)
    o_ref[...] = x * lax.rsqrt(ms + EPS) * g_ref[...]


def _final_norm(x2, g, *, tm):
    n, d = x2.shape
    assert n % tm == 0
    return pl.pallas_call(
        _final_norm_kernel,
        grid=(n // tm,),
        in_specs=[pl.BlockSpec((tm, d), lambda i: (i, 0)), pl.BlockSpec((1, d), lambda i: (0, 0))],
        out_specs=pl.BlockSpec((tm, d), lambda i: (i, 0)),
        out_shape=jax.ShapeDtypeStruct((n, d), F32),
        compiler_params=pltpu.CompilerParams(dimension_semantics=("arbitrary",)),
        name="final_norm",
    )(x2, g)


def _prefix_kernel(x_ref, sel_ref, o_ref):
    o_ref[...] = _dot3(x_ref[...], sel_ref[...])


def _prefix(x, sel, *, tm):
    n, k = x.shape
    w = sel.shape[1]
    assert n % tm == 0
    return pl.pallas_call(
        _prefix_kernel,
        grid=(n // tm,),
        in_specs=[pl.BlockSpec((tm, k), lambda i: (i, 0)), pl.BlockSpec((k, w), lambda i: (0, 0))],
        out_specs=pl.BlockSpec((tm, w), lambda i: (i, 0)),
        out_shape=jax.ShapeDtypeStruct((n, w), F32),
        compiler_params=pltpu.CompilerParams(dimension_semantics=("arbitrary",)),
        name="logf_prefix",
    )(x, sel)


def _block_diag_queries(q, n_rep, row_shift, lane_shift):
    qt = jnp.concatenate([q] * n_rep, axis=0)
    rg = lax.broadcasted_iota(jnp.int32, qt.shape, 0) >> row_shift
    lg = lax.broadcasted_iota(jnp.int32, qt.shape, 1) >> lane_shift
    return jnp.where(rg == lg, qt, 0.0).astype(BF16)


def _pad_rows(x, rows):
    return jnp.concatenate([x, jnp.zeros((rows - x.shape[0], x.shape[1]), x.dtype)], axis=0)


def _expand_heads(pieces, n_dec):
    sub = lax.broadcasted_iota(jnp.int32, (2 * n_dec, LANES), 0)
    out = []
    for p in range(len(pieces) // 2):
        a = jnp.broadcast_to(pieces[2 * p], (2 * n_dec, LANES))
        b = jnp.broadcast_to(pieces[2 * p + 1], (2 * n_dec, LANES))
        out.append(jnp.where(sub < n_dec, a, b))
    return jnp.concatenate(out, axis=0)


def _sample_stick(q, k_refs, vt_refs, kn_ref, vn_ref, acc_ref, n_dec):
    n_pages = len(k_refs)
    n_rows = HA * n_dec
    qbd = _block_diag_queries(q, HA, 2, 6)
    rowi = lax.broadcasted_iota(jnp.int32, (n_rows, LANES), 0)
    lane = lax.broadcasted_iota(jnp.int32, (n_rows, LANES), 1)
    strict = lane < (rowi & (n_dec - 1))
    key, key2 = _tile_iotas(LANES)
    later = (key > key2).astype(BF16)
    zs = [_dot(qbd, k_refs[r][0, 0].astype(BF16)) for r in range(n_pages)]
    zs.append(_dot_nt(qbd, _pad_rows(kn_ref[0], LANES).astype(BF16)))
    ls = [_log2_sigmoid(z) for z in zs]
    lk = [l - z for l, z in zip(ls, zs)]
    lk[-1] = jnp.where(strict, lk[-1], 0.0)
    his = [t.astype(BF16) for t in lk]
    los = [(t - h.astype(F32)).astype(BF16) for t, h in zip(lk, his)]
    cum = _dot(jnp.concatenate(his + los, axis=0), later)
    n_t = n_pages + 1
    cums = [cum[i * n_rows:(i + 1) * n_rows] + cum[(n_t + i) * n_rows:(n_t + i + 1) * n_rows] for i in range(n_t)]
    sums = [jnp.sum(t, axis=1, keepdims=True) for t in lk]
    w = jnp.where(strict, jnp.exp2(ls[-1] + cums[-1]), 0.0)
    acc = _dot(w.astype(BF16), _pad_rows(vn_ref[0], LANES).astype(BF16))
    c = sums[-1]
    for r in range(n_pages - 1, -1, -1):
        w = jnp.exp2(ls[r] + cums[r] + c)
        acc = acc + _dot_nt(w.astype(BF16), vt_refs[r][0, 0].astype(BF16))
        c = c + sums[r]
    acc_ref[...] = acc


def _sample_softmax(scores, vt_refs, vn_ref, acc_ref, l_ref):
    m = jnp.max(functools.reduce(jnp.maximum, scores), axis=1, keepdims=True)
    ps = [jnp.exp2(s - m) for s in scores]
    l_ref[...] = jnp.sum(functools.reduce(jnp.add, ps), axis=1, keepdims=True)
    acc = _dot(ps[-1].astype(BF16), _pad_rows(vn_ref[0], LANES).astype(BF16))
    for r, vt_ref in enumerate(vt_refs):
        acc = acc + _dot_nt(ps[r].astype(BF16), vt_ref[0, 0].astype(BF16))
    acc_ref[...] = acc


def _sample_kernel(pt_ref, q_ref, *refs, n_dec, n_pages, lam_init):
    del pt_ref
    refs = list(refs)
    take = lambda n: [refs.pop(0) for _ in range(n)]
    ka, va, kb, vb, kc, vc, fp = (take(n_pages) for _ in range(7))
    kna, vna, knb, vnb, knc, vnc, fn_ref, dl_ref, dn_ref = take(9)
    oa_ref, ob_ref, oc_ref = take(3)
    acc_a, acc_b, l_b, acc_c, l_c = take(5)
    q = q_ref[0]

    _sample_stick(q[:, 0:WA], ka, va, kna, vna, acc_a, n_dec)

    n_rows = HB * n_dec
    qbd = _block_diag_queries(q[:, WA:WA + WB], HB, 2, 6)
    rowi = lax.broadcasted_iota(jnp.int32, (n_rows, LANES), 0)
    lane = lax.broadcasted_iota(jnp.int32, (n_rows, LANES), 1)
    qpos = rowi & (n_dec - 1)
    fnew = _expand_heads([fn_ref[0][:, h * LANES:(h + 1) * LANES] for h in range(HB)], n_dec)
    cs = jnp.sum(jnp.where(lane == qpos, fnew, 0.0), axis=1, keepdims=True)
    s_new = _dot_nt(qbd, _pad_rows(knb[0], LANES).astype(BF16)) + (cs - fnew) * LOG2E
    scores = [None] * n_pages + [jnp.where(lane <= qpos, s_new, NEG_INF)]
    for r in range(n_pages - 1, -1, -1):
        f = fp[r][0, 0]
        fpg = _expand_heads([f[h:h + 1, :] for h in range(HB)], n_dec)
        cs = cs + fpg[:, LANES - 1:LANES]
        scores[r] = _dot(qbd, kb[r][0, 0].astype(BF16)) + (cs - fpg) * LOG2E
    _sample_softmax(scores, vb, vnb, acc_b, l_b)

    n_rows = HC * 2 * n_dec
    qbd = _block_diag_queries(q[:, WA + WB:], HC * 2, 2, 5)
    rowi = lax.broadcasted_iota(jnp.int32, (n_rows, LANES), 0)
    lane = lax.broadcasted_iota(jnp.int32, (n_rows, LANES), 1)
    scores = [_dot(qbd, kc[r][0, 0].astype(BF16)) for r in range(n_pages)]
    scores.append(jnp.where(lane <= (rowi & (n_dec - 1)), _dot_nt(qbd, _pad_rows(knc[0], LANES).astype(BF16)), NEG_INF))
    _sample_softmax(scores, vc, vnc, acc_c, l_c)

    laneh = lax.broadcasted_iota(jnp.int32, (n_dec, WA), 1) >> 6
    out = jnp.zeros((n_dec, WA), F32)
    for h in range(HA):
        out = jnp.where(laneh == h, acc_a[h * n_dec:(h + 1) * n_dec, :], out)
    oa_ref[0] = out

    laneh = lax.broadcasted_iota(jnp.int32, (n_dec, WB), 1) >> 6
    out = jnp.zeros((n_dec, WB), F32)
    for h in range(HB):
        rows = slice(h * n_dec, (h + 1) * n_dec)
        out = jnp.where(laneh == h, acc_b[rows, :] / l_b[rows, :], out)
    ob_ref[0] = out

    lam = _diff_lambda(dl_ref, lam_init)
    out = jnp.zeros((n_dec, WC), F32)
    for h in range(HC):
        r1 = slice((2 * h) * n_dec, (2 * h + 1) * n_dec)
        r2 = slice((2 * h + 1) * n_dec, (2 * h + 2) * n_dec)
        o = acc_c[r1, :] / l_c[r1, :] - lam * (acc_c[r2, :] / l_c[r2, :])
        mine = laneh == h
        ms = jnp.sum(jnp.where(mine, o * o, 0.0), axis=1, keepdims=True) * (1.0 / HEAD_DIM)
        out = jnp.where(mine, o * lax.rsqrt(ms + EPS), out)
    oc_ref[0] = out * dn_ref[...] * (1.0 - lam_init)


def _sample_attn(page_table, q3, caches_t, fpage, new_rows, fnew, dl, dn, layer, *, lam_init):
    nb, n_dec, d = q3.shape
    assert 2 * n_dec == SUBLANES, "two heads' query rows must fill one 8-sublane tile"
    n_pages = page_table.shape[1]
    page = caches_t[0].shape[3]
    assert page == LANES
    per_b = lambda b, pt: (b, 0, 0)
    const = lambda b, pt: (0, 0)
    in_specs = [pl.BlockSpec((1, n_dec, d), per_b)]
    args = [q3]
    for c in tuple(caches_t) + (fpage,):
        for r in range(n_pages):
            in_specs.append(pl.BlockSpec((1, 1, c.shape[2], page), lambda b, pt, r=r: (layer, pt[b, r], 0, 0)))
            args.append(c)
    for t in new_rows:
        in_specs.append(pl.BlockSpec((1, n_dec, t.shape[2]), per_b))
        args.append(t)
    in_specs += [pl.BlockSpec((1, 1, HB * LANES), per_b), pl.BlockSpec((4, QK_C), const),
                 pl.BlockSpec((1, WC), const)]
    args += [fnew, dl, dn]
    widths = (WA, WB, WC)
    scratch = [pltpu.VMEM((HA * n_dec, WA), F32),
               pltpu.VMEM((HB * n_dec, WB), F32), pltpu.VMEM((HB * n_dec, 1), F32),
               pltpu.VMEM((HC * 2 * n_dec, WC), F32), pltpu.VMEM((HC * 2 * n_dec, 1), F32)]
    grid_spec = pltpu.PrefetchScalarGridSpec(
        num_scalar_prefetch=1,
        grid=(nb,),
        in_specs=in_specs,
        out_specs=tuple(pl.BlockSpec((1, n_dec, w), per_b) for w in widths),
        scratch_shapes=scratch,
    )
    return pl.pallas_call(
        functools.partial(_sample_kernel, n_dec=n_dec, n_pages=n_pages, lam_init=lam_init),
        grid_spec=grid_spec,
        out_shape=tuple(jax.ShapeDtypeStruct((nb, n_dec, w), F32) for w in widths),
        compiler_params=pltpu.CompilerParams(dimension_semantics=("arbitrary",), vmem_limit_bytes=VMEM_LIMIT),
        name="sample_attn",
    )(page_table, *args)


def _rotate_half_columns(w):
    lead = w.shape[:-1]
    g = w.reshape(lead + (w.shape[-1] // QK_C, 2, QK_C // 2))
    return jnp.concatenate([-g[..., 1:2, :], g[..., 0:1, :]], axis=-2).reshape(w.shape)


def _arrange_w_in(w_in):
    sizes = [WA] * 3 + [WB] * 3 + [HB] + [WC] * 3
    bounds = [0]
    for s in sizes:
        bounds.append(bounds[-1] + s)
    qa, ka, va, qb, kb, vb, fb, qc, kc, vc = [w_in[..., bounds[i]:bounds[i + 1]] for i in range(len(sizes))]
    pad = jnp.zeros(w_in.shape[:-1] + (LANES - HB,), w_in.dtype)
    cols = [qa, ka, va, qb, kb, vb, qc, kc, vc, _rotate_half_columns(qc), _rotate_half_columns(kc), fb, pad]
    w_arr = jnp.concatenate(cols, axis=-1).astype(BF16)
    w_vt = jnp.swapaxes(jnp.concatenate([va, vb, vc], axis=-1), -1, -2).astype(BF16)
    return w_arr, w_vt


def _rope_tables(pos):
    half = QK_C // 2
    inv = ROPE_THETA ** (-jnp.arange(half, dtype=F32) / half)
    ang = pos.astype(F32)[:, None] * inv[None, :]
    reps = WC // half
    return jnp.tile(jnp.cos(ang), (1, reps)), jnp.tile(jnp.sin(ang), (1, reps))


def _new_prefix_matrix(n_dec, n_heads):
    r = jnp.arange(n_dec * LANES)
    c = jnp.arange(n_heads * LANES)
    m = (((r % LANES)[:, None] == (c // LANES)[None, :]) & ((r // LANES)[:, None] <= (c % LANES)[None, :])
         & ((c % LANES)[None, :] < n_dec))
    return m.astype(BF16)


def _keys_on_lanes(cache):
    l, pool, page, h, hd = cache.shape
    return jnp.transpose(cache, (0, 1, 3, 4, 2)).reshape(l, pool, h * hd, page)


def kernel(x_prompt, x_sample, cache_a_k, cache_a_v, cache_b_k, cache_b_v, cache_b_logf, cache_c_k, cache_c_v,
           page_table, w_in, b_forget, w_out, g_attn, diff_lambda, diff_norm, g_ffn, w_group, b_group,
           w_expert_router, b_expert_router, w_up_gate, w_up, w_down, g_final):
    nb, nt, d = x_prompt.shape
    sb, n_dec, _ = x_sample.shape
    depth = w_in.shape[0]
    n_pool, page = cache_a_k.shape[1], cache_a_k.shape[2]
    n_pages = page_table.shape[1]
    assert page == LANES
    tm = min(512, nt)
    tq = min(256, nt)
    tq_ab = min(512, nt)

    w_arr, w_vt = _arrange_w_in(w_in)
    bf_pad = jnp.pad(b_forget.astype(F32), ((0, 0), (0, LANES - HB)))[:, None, :]
    wo = w_out.astype(BF16)
    wr_t = jnp.zeros((depth, _R_ROWS, d), F32)
    wr_t = wr_t.at[:, 0:N_GROUPS].set(jnp.swapaxes(w_group, 1, 2).astype(F32))
    wr_t = wr_t.at[:, _R_EXP0:_R_EXP0 + N_GROUPS * EXPERTS_PER_GROUP].set(
        jnp.swapaxes(w_expert_router, 1, 2).astype(F32))
    wr_hi = wr_t.astype(BF16)
    wr_t = jnp.concatenate([wr_hi, (wr_t - wr_hi.astype(F32)).astype(BF16)], axis=1)
    br = jnp.zeros((depth, _R_ROWS), F32)
    br = br.at[:, 0:N_GROUPS].set(b_group.astype(F32))
    br = br.at[:, _R_EXP0:_R_EXP0 + N_GROUPS * EXPERTS_PER_GROUP].set(b_expert_router.astype(F32))[:, :, None]
    w1 = w_up_gate.astype(BF16)
    w3 = w_up.astype(BF16)
    w2 = w_down.astype(BF16)
    dn_col = diff_norm.astype(F32)[:, :, None]
    dn_row = jnp.tile(diff_norm.astype(F32), (1, HC))[:, None, :]
    dl = diff_lambda.astype(F32)

    cos_p, sin_p = _rope_tables(jnp.arange(nt))
    pos_s = n_pages * page + jnp.arange(n_dec)
    cos_s, sin_s = _rope_tables(jnp.tile(pos_s, sb))

    caches_t = tuple(_keys_on_lanes(c.astype(F32)) for c in
                     (cache_a_k, cache_a_v, cache_b_k, cache_b_v, cache_c_k, cache_c_v))
    lf_rows = jnp.transpose(cache_b_logf.astype(F32), (0, 3, 1, 2)).reshape(depth * HB * n_pool, page)
    key, key2 = jnp.arange(page)[:, None], jnp.arange(page)[None, :]
    fpage = _prefix(lf_rows, (key <= key2).astype(BF16), tm=math.gcd(depth * HB * n_pool, 4096))
    fpage = jnp.transpose(fpage.reshape(depth, HB, n_pool, page), (0, 2, 1, 3))
    sel_new = _new_prefix_matrix(n_dec, HB)

    xp = x_prompt.reshape(nb * nt, d).astype(F32)
    xs = x_sample.reshape(sb * n_dec, d).astype(F32)
    ns = sb * n_dec
    rows_p = None
    rows_s = [[] for _ in range(7)]
    for l in range(depth):
        lam_init = 0.8 - 0.6 * math.exp(-0.3 * l)
        g_a = g_attn[l].astype(F32)[None, :]
        g_f = g_ffn[l].astype(F32)[None, :]

        outs = _inproj(xp, g_a, w_arr[l], w_vt[l], bf_pad[l], cos_p, sin_p, tm=tm, n_pos_tiles=nt // tm,
                       n_batch=nb, layer=l, depth=depth, stacked=rows_p)
        q, lfp = outs[0], outs[6]
        rows_p = tuple(outs[i] for i in _STACKED)
        ka, vat, kb, vbt, _, kc, vct = rows_p
        b3 = lambda t: t.reshape(nb, nt, t.shape[-1])
        b4 = lambda t: t.reshape(depth, nb, nt, t.shape[-1])
        q3 = b3(q)
        oa = _prompt_attn("a", q3, b4(ka), vat, (), l, tq=tq_ab)
        ob = _prompt_attn("b", q3, b4(kb), vbt, (b3(lfp),), l, tq=tq_ab)
        oc = _prompt_attn("c", q3, b4(kc), vct, (dl[l], dn_col[l]), l, tq=tq, lam_init=lam_init)
        flat = lambda t: t.reshape(nb * nt, t.shape[-1])
        xp = _outproj_moe(xp, flat(oa), flat(ob), flat(oc), wo[l], g_f, wr_t[l], br[l], w1, w3, w2, l, tm=tm)

        q, ka, va, kb, vb, lf, lfp, kc, vc = _inproj(xs, g_a, w_arr[l], None, bf_pad[l], cos_s, sin_s,
                                                     tm=ns, n_pos_tiles=1)
        s3 = lambda t: t.reshape(sb, n_dec, t.shape[-1])
        fnew = _prefix(lfp.reshape(sb, n_dec * LANES), sel_new, tm=sb).reshape(sb, 1, HB * LANES)
        oa, ob, oc = _sample_attn(page_table, s3(q.astype(F32)), caches_t, fpage,
                                  tuple(s3(t) for t in (ka, va, kb, vb, kc, vc)), fnew, dl[l], dn_row[l], l,
                                  lam_init=lam_init)
        sflat = lambda t: t.reshape(ns, t.shape[-1]).astype(BF16)
        xs = _outproj_moe(xs, sflat(oa), sflat(ob), sflat(oc), wo[l], g_f, wr_t[l], br[l], w1, w3, w2, l, tm=ns)
        for i, t in enumerate((ka, va, kb, vb, lf, kc, vc)):
            rows_s[i].append(t)

    g_fin = g_final.astype(F32)[None, :]
    y_prompt = _final_norm(xp, g_fin, tm=tm).reshape(nb, nt, d)
    y_sample = _final_norm(xs, g_fin, tm=ns).reshape(sb, n_dec, d)

    def heads(arrays, b, t, transposed):
        outs = []
        for i, a in enumerate(arrays):
            if i == 4:
                outs.append(a.reshape(depth, b, t, HB))
            elif transposed and i in (1, 3, 6):
                a = a.reshape(depth, b, a.shape[2] // HEAD_DIM, HEAD_DIM, t)
                outs.append(jnp.transpose(a, (0, 1, 4, 2, 3)))
            else:
                outs.append(a.reshape(depth, b, t, a.shape[-1] // HEAD_DIM, HEAD_DIM))
        return outs

    rows_s = [jnp.stack(r, axis=0) for r in rows_s]
    return (y_prompt, y_sample, *heads(rows_p, nb, nt, True), *heads(rows_s, sb, n_dec, False))
```
